```python
import math
import jax, jax.numpy as jnp
from jax import lax
import numpy as np

D_MODEL = 1024
BATCH = 4
SEQ = 4096
DEPTH = 1
DEC_BATCH = 8
DEC_SEQ = 4096
PAST_LEN = 128

HEAD_DIM = 64
N_HEADS = 8
N_KV = 2
GROUP = N_HEADS // N_KV
ATT_WIDTH = N_HEADS * HEAD_DIM
KV_WIDTH = N_KV * HEAD_DIM
GM_HEADS = 8
GM_HD = 64
GM_WIDTH = GM_HEADS * GM_HD
MIX_WIDTH = ATT_WIDTH + GM_WIDTH
IN_WIDTH = ATT_WIDTH + 2 * KV_WIDTH + 2 * GM_WIDTH
BLK = 128
WINDOW = 128
CHUNK = 128
N_BUCKETS = 32
MAX_DIST = 128
D_FF = int(math.ceil(8 * D_MODEL / 3 / 256) * 256)
EPS = 1e-6
NEG = -1e30

kernel_name = "hymba_style_window_gqa_gmlp_encoder"


def rms_norm(x, g):
    xf = x.astype(jnp.float32)
    y = xf * lax.rsqrt(jnp.mean(xf * xf, axis=-1, keepdims=True) + EPS)
    return (y * g.astype(jnp.float32)).astype(x.dtype)


def t5_buckets(rel):
    half = N_BUCKETS // 2
    max_exact = half // 2
    ret = (rel > 0).astype(np.int32) * half
    n = np.abs(rel)
    large = max_exact + (np.log(np.maximum(n, 1).astype(np.float32) / max_exact)
                         / np.log(MAX_DIST / max_exact) * (half - max_exact)).astype(np.int32)
    large = np.minimum(large, half - 1)
    return (ret + np.where(n < max_exact, n, large)).astype(np.int32)


def windowed_gqa(q, k, v, q_gain, k_gain, sink, rel_table):
    B, S, _ = q.shape
    nb = S // BLK
    q = rms_norm(q.reshape(B, S, N_KV, GROUP, HEAD_DIM), q_gain)
    k = rms_norm(k.reshape(B, S, N_KV, HEAD_DIM), k_gain)
    v = v.reshape(B, S, N_KV, HEAD_DIM)
    pad = ((0, 0), (BLK, BLK), (0, 0), (0, 0))
    kp = jnp.pad(k, pad).reshape(B, nb + 2, BLK, N_KV, HEAD_DIM)
    vp = jnp.pad(v, pad).reshape(B, nb + 2, BLK, N_KV, HEAD_DIM)
    kband = jnp.concatenate([kp[:, :-2], kp[:, 1:-1], kp[:, 2:]], axis=2)
    vband = jnp.concatenate([vp[:, :-2], vp[:, 1:-1], vp[:, 2:]], axis=2)
    qb = q.reshape(B, nb, BLK, N_KV, GROUP, HEAD_DIM)
    scores = jnp.einsum("bnqkgd,bnskd->bnkgqs", qb, kband).astype(jnp.float32)
    scores = scores * (1.0 / math.sqrt(HEAD_DIM))
    rel = (np.arange(3 * BLK) - BLK)[None, :] - np.arange(BLK)[:, None]
    bias = rel_table.astype(jnp.float32)[t5_buckets(rel)]
    bias = jnp.transpose(bias, (2, 0, 1)).reshape(N_KV, GROUP, BLK, 3 * BLK)
    band = np.abs(rel) <= WINDOW
    key_pos = np.arange(nb)[:, None] * BLK - BLK + np.arange(3 * BLK)[None, :]
    valid = (key_pos >= 0) & (key_pos < S)
    mask = band[None] & valid[:, None, :]
    scores = jnp.where(mask[None, :, None, None], scores + bias[None, None], NEG)
    s_sink = sink.astype(jnp.float32).reshape(N_KV, GROUP)[None, None, :, :, None, None]
    m = jnp.maximum(jnp.max(scores, axis=-1, keepdims=True), s_sink)
    e = jnp.exp(scores - m)
    probs = e / (jnp.sum(e, axis=-1, keepdims=True) + jnp.exp(s_sink - m))
    out = jnp.einsum("bnkgqs,bnskd->bnqkgd", probs.astype(v.dtype), vband)
    return out.reshape(B, S, ATT_WIDTH)


def chunked_spatial_gating(u, v, v_gain, w_s, b_s):
    B, S, _ = u.shape
    nc = S // CHUNK
    u = jax.nn.gelu(u)
    v = rms_norm(jax.nn.gelu(v), v_gain)
    vh = v.reshape(B, nc, CHUNK, GM_HEADS, GM_HD)
    s = jnp.einsum("hpq,bcqhd->bcphd", w_s, vh) + jnp.transpose(b_s)[None, None, :, :, None]
    return (u.reshape(B, nc, CHUNK, GM_HEADS, GM_HD) * s).reshape(B, S, GM_WIDTH)


def encoder_layer(x, rel_table, norm1, w_in, q_gain, k_gain, sink, v_gain, w_s, b_s,
                  attn_out_gain, gmlp_out_gain, w_o, norm2, w_gate, w_up, w_down):
    h = rms_norm(x, norm1)
    proj = h @ w_in
    o1 = ATT_WIDTH
    o2 = o1 + KV_WIDTH
    o3 = o2 + KV_WIDTH
    o4 = o3 + GM_WIDTH
    q, k, v = proj[..., :o1], proj[..., o1:o2], proj[..., o2:o3]
    gu, gv = proj[..., o3:o4], proj[..., o4:]
    a = rms_norm(windowed_gqa(q, k, v, q_gain, k_gain, sink, rel_table), attn_out_gain)
    g = rms_norm(chunked_spatial_gating(gu, gv, v_gain, w_s, b_s), gmlp_out_gain)
    x = x + jnp.concatenate([a, g], axis=-1) @ w_o
    h2 = rms_norm(x, norm2)
    x = x + (jax.nn.silu(h2 @ w_gate) * (h2 @ w_up)) @ w_down
    return x


def setup_inputs(seed: int = 0) -> dict:
    key = jax.random.key(seed)
    ks = jax.random.split(key, 20)
    f = jnp.float32
    nrm = lambda k, shape, s: jax.random.normal(k, shape, f) * s
    L = DEPTH
    return {
        "x_prompt": nrm(ks[0], (BATCH, SEQ, D_MODEL), 1.0),
        "x_sample": nrm(ks[1], (DEC_BATCH, DEC_SEQ, D_MODEL), 1.0),
        "rel_bias_table": nrm(ks[2], (N_BUCKETS, N_HEADS), 0.5),
        "norm1": 1.0 + nrm(ks[3], (L, D_MODEL), 0.05),
        "w_in": nrm(ks[4], (L, D_MODEL, IN_WIDTH), D_MODEL ** -0.5),
        "q_gain": 1.0 + nrm(ks[5], (L, HEAD_DIM), 0.05),
        "k_gain": 1.0 + nrm(ks[6], (L, HEAD_DIM), 0.05),
        "sink": nrm(ks[7], (L, N_HEADS), 0.5),
        "v_gain": 1.0 + nrm(ks[8], (L, GM_WIDTH), 0.05),
        "w_s": nrm(ks[9], (L, GM_HEADS, CHUNK, CHUNK), CHUNK ** -0.5),
        "b_s": 1.0 + nrm(ks[10], (L, GM_HEADS, CHUNK), 0.1),
        "attn_out_gain": 1.0 + nrm(ks[11], (L, ATT_WIDTH), 0.05),
        "gmlp_out_gain": 1.0 + nrm(ks[12], (L, GM_WIDTH), 0.05),
        "w_o": nrm(ks[13], (L, MIX_WIDTH, D_MODEL), MIX_WIDTH ** -0.5),
        "norm2": 1.0 + nrm(ks[14], (L, D_MODEL), 0.05),
        "w_gate": nrm(ks[15], (L, D_MODEL, D_FF), D_MODEL ** -0.5),
        "w_up": nrm(ks[16], (L, D_MODEL, D_FF), D_MODEL ** -0.5),
        "w_down": nrm(ks[17], (L, D_FF, D_MODEL), D_FF ** -0.5),
    }


def reference(x_prompt, x_sample, rel_bias_table, norm1, w_in, q_gain, k_gain, sink, v_gain,
              w_s, b_s, attn_out_gain, gmlp_out_gain, w_o, norm2, w_gate, w_up, w_down):
    y_prompt = x_prompt
    y_sample = x_sample
    for l in range(DEPTH):
        params = (rel_bias_table, norm1[l], w_in[l], q_gain[l], k_gain[l], sink[l], v_gain[l],
                  w_s[l], b_s[l], attn_out_gain[l], gmlp_out_gain[l], w_o[l], norm2[l],
                  w_gate[l], w_up[l], w_down[l])
        y_prompt = encoder_layer(y_prompt, *params)
        y_sample = encoder_layer(y_sample, *params)
    return (y_prompt, y_sample)
```

```python
import functools
import math

import jax
import jax.numpy as jnp
import numpy as np
from jax import lax
from jax.experimental import pallas as pl
from jax.experimental.pallas import tpu as pltpu

D_MODEL = 1024
HEAD_DIM = 64
N_HEADS = 8
N_KV = 2
ATT_WIDTH = N_HEADS * HEAD_DIM
KV_WIDTH = N_KV * HEAD_DIM
GM_HEADS = 8
GM_WIDTH = GM_HEADS * HEAD_DIM
IN_WIDTH = ATT_WIDTH + 2 * KV_WIDTH + 2 * GM_WIDTH
BLK = 128
N_BUCKETS = 32
MAX_DIST = 128
EPS = 1e-6
NEG = -1e30

LANES = 128
PAIRS = ATT_WIDTH // LANES
MIX_TOKENS = 512
FFN_TOKENS = 256
VMEM_LIMIT_BYTES = 56 * 1024 * 1024

_Q_OFF = 0
_KV_OFF = ATT_WIDTH
_G_OFF = ATT_WIDTH + 2 * KV_WIDTH


def _t5_bucket_table():
    rel = (np.arange(3 * BLK) - BLK)[None, :] - np.arange(BLK)[:, None]
    half = N_BUCKETS // 2
    max_exact = half // 2
    ret = (rel > 0).astype(np.int32) * half
    n = np.abs(rel)
    large = max_exact + (np.log(np.maximum(n, 1).astype(np.float32) / max_exact)
                         / np.log(MAX_DIST / max_exact) * (half - max_exact)).astype(np.int32)
    large = np.minimum(large, half - 1)
    bucket = (ret + np.where(n < max_exact, n, large)).astype(np.int32)
    return np.where(np.abs(rel) <= BLK, bucket, -1).astype(np.int32)


def _row_rms(x, gain):
    ms = jnp.mean(x * x, axis=-1, keepdims=True)
    return x * lax.rsqrt(ms + EPS) * gain


def _head_rms(x, gain, low_half):
    x2 = x * x
    s_lo = jnp.sum(jnp.where(low_half, x2, 0.0), axis=-1, keepdims=True)
    s_hi = jnp.sum(jnp.where(low_half, 0.0, x2), axis=-1, keepdims=True)
    ms = jnp.where(low_half, s_lo, s_hi) * (1.0 / HEAD_DIM)
    return x * lax.rsqrt(ms + EPS) * gain


def _mixer_kernel(table_ref, sink_ref, bucket_ref, xc_ref, xp_ref, xn_ref, norm1_ref, w_in_ref,
                  q_gain_ref, k_gain_ref, v_gain_ref, w_s_ref, b_s_ref, a_gain_ref, g_gain_ref,
                  w_o_ref, out_ref, bias_sc, q_sc, kz_sc, vz_sc, att_sc, gm_sc):
    tb = xc_ref.shape[1]
    n_blk = tb // BLK
    step = pl.program_id(1)
    n_steps = pl.num_programs(1)
    f32 = jnp.float32
    bf16 = jnp.bfloat16

    lane = lax.broadcasted_iota(jnp.int32, (1, LANES), 1)
    low_half = lane < HEAD_DIM

    @pl.when(jnp.logical_and(pl.program_id(0) == 0, step == 0))
    def _build_bias():
        for h in range(N_HEADS):
            for part in range(3):
                bk = bucket_ref[:, part * BLK:(part + 1) * BLK]

                def body(b, acc):
                    return jnp.where(bk == b, table_ref[b, h], acc)

                acc = lax.fori_loop(0, N_BUCKETS, body, jnp.full((BLK, BLK), NEG, f32))
                neg = jnp.full((BLK, BLK), NEG, f32)
                sl = slice(part * BLK, (part + 1) * BLK)
                bias_sc[0, h, :, sl] = acc
                bias_sc[1, h, :, sl] = neg if part == 0 else acc
                bias_sc[2, h, :, sl] = neg if part == 2 else acc

    n1 = norm1_ref[...]
    hc = _row_rms(xc_ref[0], n1).astype(bf16)
    hp = _row_rms(xp_ref[0], n1).astype(bf16)
    hn = _row_rms(xn_ref[0], n1).astype(bf16)
    he = jnp.concatenate([hp, hc, hn], axis=0)

    kv = jnp.dot(he, w_in_ref[:, _KV_OFF:_KV_OFF + 2 * KV_WIDTH], preferred_element_type=f32)
    kn = _head_rms(kv[:, :KV_WIDTH], k_gain_ref[...], low_half)
    vv = kv[:, KV_WIDTH:]
    kn_sw = pltpu.roll(kn, HEAD_DIM, axis=1)
    vv_sw = pltpu.roll(vv, HEAD_DIM, axis=1)
    zero = jnp.zeros_like(kn)
    for idx, (src_k, src_v, keep_low) in enumerate(
            ((kn, vv, True), (kn_sw, vv_sw, False), (kn_sw, vv_sw, True), (kn, vv, False))):
        keep = low_half if keep_low else jnp.logical_not(low_half)
        kz_sc[idx] = jnp.where(keep, src_k, zero).astype(bf16)
        vz_sc[idx] = jnp.where(keep, src_v, zero).astype(bf16)

    qf = jnp.dot(hc, w_in_ref[:, _Q_OFF:_Q_OFF + ATT_WIDTH], preferred_element_type=f32)
    scale = 1.0 / math.sqrt(HEAD_DIM)
    for c in range(PAIRS):
        sl = slice(c * LANES, (c + 1) * LANES)
        q_sc[:, sl] = (_head_rms(qf[:, sl], q_gain_ref[:, sl], low_half) * scale).astype(bf16)

    gf = jnp.dot(hc, w_in_ref[:, _G_OFF:_G_OFF + 2 * GM_WIDTH], preferred_element_type=f32)
    gu = jax.nn.gelu(gf[:, :GM_WIDTH])
    gv = _row_rms(jax.nn.gelu(gf[:, GM_WIDTH:]), v_gain_ref[...])
    for r in range(n_blk):
        rows = slice(r * BLK, (r + 1) * BLK)
        for c in range(PAIRS):
            sl = slice(c * LANES, (c + 1) * LANES)
            vt = gv[rows, sl]
            stacked = jnp.concatenate(
                [jnp.where(low_half, vt, 0.0), jnp.where(low_half, 0.0, vt)], axis=0).astype(bf16)
            s = jnp.dot(w_s_ref[c], stacked, preferred_element_type=f32) + b_s_ref[c]
            gm_sc[rows, sl] = gu[rows, sl] * s

    first_blk = step == 0
    last_step = step == n_steps - 1

    def attend(qb, carry):
        r0 = pl.multiple_of(qb * BLK, BLK)
        is_first = jnp.logical_and(first_blk, qb == 0)
        is_last = jnp.logical_and(last_step, qb == n_blk - 1)
        variant = jnp.where(is_first, 1, jnp.where(is_last, 2, 0))
        for j in range(N_KV):
            kz = jnp.concatenate([kz_sc[2 * j, pl.ds(r0, 3 * BLK), :],
                                  kz_sc[2 * j + 1, pl.ds(r0, 3 * BLK), :]], axis=0)
            vz = jnp.concatenate([vz_sc[2 * j, pl.ds(r0, 3 * BLK), :],
                                  vz_sc[2 * j + 1, pl.ds(r0, 3 * BLK), :]], axis=0)
            for cl in range(PAIRS // N_KV):
                c = j * (PAIRS // N_KV) + cl
                qc = q_sc[pl.ds(r0, BLK), c * LANES:(c + 1) * LANES]
                s2 = lax.dot_general(qc, kz, (((1,), (1,)), ((), ())),
                                     preferred_element_type=f32)
                es = []
                invs = []
                for p in range(2):
                    h = 2 * c + p
                    s = s2[:, p * 3 * BLK:(p + 1) * 3 * BLK] + bias_sc[variant, h]
                    sink = sink_ref[h]
                    m = jnp.maximum(jnp.max(s, axis=-1, keepdims=True), sink)
                    e = jnp.exp(s - m)
                    denom = jnp.sum(e, axis=-1, keepdims=True) + jnp.exp(sink - m)
                    es.append(e.astype(bf16))
                    invs.append(1.0 / denom)
                pv = jnp.dot(jnp.concatenate(es, axis=1), vz, preferred_element_type=f32)
                att_sc[pl.ds(r0, BLK), c * LANES:(c + 1) * LANES] = pv * jnp.where(
                    low_half, invs[0], invs[1])
        return carry

    lax.fori_loop(0, n_blk, attend, 0)

    a = _row_rms(att_sc[...], a_gain_ref[...]).astype(bf16)
    g = _row_rms(gm_sc[...], g_gain_ref[...]).astype(bf16)
    y = jnp.dot(a, w_o_ref[:ATT_WIDTH, :], preferred_element_type=f32)
    y = y + jnp.dot(g, w_o_ref[ATT_WIDTH:, :], preferred_element_type=f32)
    out_ref[0] = xc_ref[0] + y


def _ffn_kernel(x_ref, norm2_ref, w_gate_ref, w_up_ref, w_down_ref, out_ref):
    f32 = jnp.float32
    x = x_ref[...]
    h = _row_rms(x, norm2_ref[...]).astype(jnp.bfloat16)
    gate = jnp.dot(h, w_gate_ref[...], preferred_element_type=f32)
    up = jnp.dot(h, w_up_ref[...], preferred_element_type=f32)
    act = (gate * (1.0 / (1.0 + jnp.exp(-gate))) * up).astype(jnp.bfloat16)
    out_ref[...] = x + jnp.dot(act, w_down_ref[...], preferred_element_type=f32)


def _resident(shape):
    zeros = (0,) * len(shape)
    return pl.BlockSpec(shape, lambda *_: zeros, pipeline_mode=pl.Buffered(1))


def _mixer(x, p):
    batch, seq, d = x.shape
    tb = MIX_TOKENS
    assert seq % tb == 0 and tb % BLK == 0 and seq // BLK >= 2 and d == D_MODEL
    r = tb // BLK
    nb = seq // BLK
    smem = pl.BlockSpec(memory_space=pltpu.SMEM)
    in_specs = [
        smem,
        smem,
        _resident((BLK, 3 * BLK)),
        pl.BlockSpec((1, tb, d), lambda b, i: (b, i, 0)),
        pl.BlockSpec((1, BLK, d), lambda b, i: (b, jnp.maximum(i * r - 1, 0), 0)),
        pl.BlockSpec((1, BLK, d), lambda b, i: (b, jnp.minimum((i + 1) * r, nb - 1), 0)),
        _resident((1, d)),
        _resident((d, IN_WIDTH)),
        _resident((1, ATT_WIDTH)),
        _resident((1, KV_WIDTH)),
        _resident((1, GM_WIDTH)),
        _resident((PAIRS, BLK, 2 * BLK)),
        _resident((PAIRS, BLK, LANES)),
        _resident((1, ATT_WIDTH)),
        _resident((1, GM_WIDTH)),
        _resident((d, d)),
    ]
    scratch = [
        pltpu.VMEM((3, N_HEADS, BLK, 3 * BLK), jnp.float32),
        pltpu.VMEM((tb, ATT_WIDTH), jnp.bfloat16),
        pltpu.VMEM((2 * N_KV, tb + 2 * BLK, LANES), jnp.bfloat16),
        pltpu.VMEM((2 * N_KV, tb + 2 * BLK, LANES), jnp.bfloat16),
        pltpu.VMEM((tb, ATT_WIDTH), jnp.float32),
        pltpu.VMEM((tb, GM_WIDTH), jnp.float32),
    ]
    return pl.pallas_call(
        _mixer_kernel,
        grid=(batch, seq // tb),
        in_specs=in_specs,
        out_specs=pl.BlockSpec((1, tb, d), lambda b, i: (b, i, 0)),
        out_shape=jax.ShapeDtypeStruct(x.shape, x.dtype),
        scratch_shapes=scratch,
        compiler_params=pltpu.CompilerParams(
            dimension_semantics=("arbitrary", "arbitrary"),
            vmem_limit_bytes=VMEM_LIMIT_BYTES),
        name="mixer",
    )(p["table"], p["sink"], p["bucket"], x, x, x, p["norm1"], p["w_in"], p["q_gain"],
      p["k_gain"], p["v_gain"], p["w_s"], p["b_s"], p["a_gain"], p["g_gain"], p["w_o"])


def _ffn(x2d, p):
    n, d = x2d.shape
    tm = FFN_TOKENS
    assert n % tm == 0
    d_ff = p["w_gate"].shape[1]
    return pl.pallas_call(
        _ffn_kernel,
        grid=(n // tm,),
        in_specs=[
            pl.BlockSpec((tm, d), lambda i: (i, 0)),
            _resident((1, d)),
            _resident((d, d_ff)),
            _resident((d, d_ff)),
            _resident((d_ff, d)),
        ],
        out_specs=pl.BlockSpec((tm, d), lambda i: (i, 0)),
        out_shape=jax.ShapeDtypeStruct(x2d.shape, x2d.dtype),
        compiler_params=pltpu.CompilerParams(
            dimension_semantics=("arbitrary",),
            vmem_limit_bytes=VMEM_LIMIT_BYTES),
        name="ffn",
    )(x2d, p["norm2"], p["w_gate"], p["w_up"], p["w_down"])


def _layer_params(rel_bias_table, norm1, w_in, q_gain, k_gain, sink, v_gain, w_s, b_s,
                  attn_out_gain, gmlp_out_gain, w_o, norm2, w_gate, w_up, w_down):
    bf16 = jnp.bfloat16
    row = lambda v: v.reshape(1, -1).astype(jnp.float32)
    w_pairs = jnp.concatenate([w_s[0::2], w_s[1::2]], axis=-1).astype(bf16)
    b_pairs = jnp.repeat(jnp.transpose(b_s.reshape(PAIRS, 2, BLK), (0, 2, 1)), HEAD_DIM, axis=-1)
    return {
        "table": rel_bias_table.astype(jnp.float32),
        "sink": sink.astype(jnp.float32),
        "bucket": jnp.asarray(_t5_bucket_table()),
        "norm1": row(norm1),
        "w_in": w_in.astype(bf16),
        "q_gain": row(jnp.tile(q_gain, N_HEADS)),
        "k_gain": row(jnp.tile(k_gain, N_KV)),
        "v_gain": row(v_gain),
        "w_s": w_pairs,
        "b_s": b_pairs.astype(jnp.float32),
        "a_gain": row(attn_out_gain),
        "g_gain": row(gmlp_out_gain),
        "w_o": w_o.astype(bf16),
        "norm2": row(norm2),
        "w_gate": w_gate.astype(bf16),
        "w_up": w_up.astype(bf16),
        "w_down": w_down.astype(bf16),
    }


def kernel(x_prompt, x_sample, rel_bias_table, norm1, w_in, q_gain, k_gain, sink, v_gain, w_s, b_s,
           attn_out_gain, gmlp_out_gain, w_o, norm2, w_gate, w_up, w_down):
    layers = [
        _layer_params(rel_bias_table, norm1[l], w_in[l], q_gain[l], k_gain[l], sink[l], v_gain[l],
                      w_s[l], b_s[l], attn_out_gain[l], gmlp_out_gain[l], w_o[l], norm2[l],
                      w_gate[l], w_up[l], w_down[l])
        for l in range(norm1.shape[0])]
    outs = []
    for x in (x_prompt, x_sample):
        for p in layers:
            x = _mixer(x, p)
            x = _ffn(x.reshape(-1, D_MODEL), p).reshape(x.shape)
        outs.append(x)
    return tuple(outs)
```

```python
import functools
import math

import jax
import jax.numpy as jnp
import numpy as np
from jax import lax
from jax.experimental import pallas as pl
from jax.experimental.pallas import tpu as pltpu

D_MODEL = 1024
HEAD_DIM = 64
N_HEADS = 8
N_KV = 2
ATT_WIDTH = N_HEADS * HEAD_DIM
KV_WIDTH = N_KV * HEAD_DIM
GM_HEADS = 8
GM_WIDTH = GM_HEADS * HEAD_DIM
IN_WIDTH = ATT_WIDTH + 2 * KV_WIDTH + 2 * GM_WIDTH
BLK = 128
N_BUCKETS = 32
MAX_DIST = 128
EPS = 1e-6
NEG = -1e30

LANES = 128
PAIRS = ATT_WIDTH // LANES
MIX_TOKENS = 512
FFN_TOKENS = 256
VMEM_LIMIT_BYTES = 56 * 1024 * 1024

_Q_OFF = 0
_KV_OFF = ATT_WIDTH
_G_OFF = ATT_WIDTH + 2 * KV_WIDTH


def _t5_bucket_table():
    rel = (np.arange(3 * BLK) - BLK)[None, :] - np.arange(BLK)[:, None]
    half = N_BUCKETS // 2
    max_exact = half // 2
    ret = (rel > 0).astype(np.int32) * half
    n = np.abs(rel)
    large = max_exact + (np.log(np.maximum(n, 1).astype(np.float32) / max_exact)
                         / np.log(MAX_DIST / max_exact) * (half - max_exact)).astype(np.int32)
    large = np.minimum(large, half - 1)
    bucket = (ret + np.where(n < max_exact, n, large)).astype(np.int32)
    return np.where(np.abs(rel) <= BLK, bucket, -1).astype(np.int32)


def _row_rms(x, gain):
    ms = jnp.mean(x * x, axis=-1, keepdims=True)
    return x * lax.rsqrt(ms + EPS) * gain


def _head_rms(x, gain, low_half):
    x2 = x * x
    s_lo = jnp.sum(jnp.where(low_half, x2, 0.0), axis=-1, keepdims=True)
    s_hi = jnp.sum(jnp.where(low_half, 0.0, x2), axis=-1, keepdims=True)
    ms = jnp.where(low_half, s_lo, s_hi) * (1.0 / HEAD_DIM)
    return x * lax.rsqrt(ms + EPS) * gain


def _mixer_kernel(table_ref, sink_ref, bucket_ref, xc_ref, xp_ref, xn_ref, norm1_ref, w_in_ref,
                  q_gain_ref, k_gain_ref, v_gain_ref, w_s_ref, b_s_ref, a_gain_ref, g_gain_ref,
                  w_o_ref, out_ref, bias_sc, q_sc, kz_sc, vz_sc, att_sc, gm_sc):
    tb = xc_ref.shape[1]
    n_blk = tb // BLK
    step = pl.program_id(1)
    n_steps = pl.num_programs(1)
    f32 = jnp.float32
    bf16 = jnp.bfloat16

    lane = lax.broadcasted_iota(jnp.int32, (1, LANES), 1)
    low_half = lane < HEAD_DIM

    @pl.when(jnp.logical_and(pl.program_id(0) == 0, step == 0))
    def _build_bias():
        for h in range(N_HEADS):
            for part in range(3):
                bk = bucket_ref[:, part * BLK:(part + 1) * BLK]

                def body(b, acc):
                    return jnp.where(bk == b, table_ref[b, h], acc)

                acc = lax.fori_loop(0, N_BUCKETS, body, jnp.full((BLK, BLK), NEG, f32))
                neg = jnp.full((BLK, BLK), NEG, f32)
                sl = slice(part * BLK, (part + 1) * BLK)
                bias_sc[0, h, :, sl] = acc
                bias_sc[1, h, :, sl] = neg if part == 0 else acc
                bias_sc[2, h, :, sl] = neg if part == 2 else acc

    n1 = norm1_ref[...]
    hc = _row_rms(xc_ref[0], n1).astype(bf16)
    hp = _row_rms(xp_ref[0], n1).astype(bf16)
    hn = _row_rms(xn_ref[0], n1).astype(bf16)
    he = jnp.concatenate([hp, hc, hn], axis=0)

    kv = jnp.dot(he, w_in_ref[:, _KV_OFF:_KV_OFF + 2 * KV_WIDTH], preferred_element_type=f32)
    kn = _head_rms(kv[:, :KV_WIDTH], k_gain_ref[...], low_half)
    vv = kv[:, KV_WIDTH:]
    kn_sw = pltpu.roll(kn, HEAD_DIM, axis=1)
    vv_sw = pltpu.roll(vv, HEAD_DIM, axis=1)
    zero = jnp.zeros_like(kn)
    for idx, (src_k, src_v, keep_low) in enumerate(
            ((kn, vv, True), (kn_sw, vv_sw, False), (kn_sw, vv_sw, True), (kn, vv, False))):
        keep = low_half if keep_low else jnp.logical_not(low_half)
        kz_sc[idx] = jnp.where(keep, src_k, zero).astype(bf16)
        vz_sc[idx] = jnp.where(keep, src_v, zero).astype(bf16)

    qf = jnp.dot(hc, w_in_ref[:, _Q_OFF:_Q_OFF + ATT_WIDTH], preferred_element_type=f32)
    scale = 1.0 / math.sqrt(HEAD_DIM)
    for c in range(PAIRS):
        sl = slice(c * LANES, (c + 1) * LANES)
        q_sc[:, sl] = (_head_rms(qf[:, sl], q_gain_ref[:, sl], low_half) * scale).astype(bf16)

    gf = jnp.dot(hc, w_in_ref[:, _G_OFF:_G_OFF + 2 * GM_WIDTH], preferred_element_type=f32)
    gu = jax.nn.gelu(gf[:, :GM_WIDTH])
    gv = _row_rms(jax.nn.gelu(gf[:, GM_WIDTH:]), v_gain_ref[...])
    for r in range(n_blk):
        rows = slice(r * BLK, (r + 1) * BLK)
        for c in range(PAIRS):
            sl = slice(c * LANES, (c + 1) * LANES)
            vt = gv[rows, sl]
            stacked = jnp.concatenate(
                [jnp.where(low_half, vt, 0.0), jnp.where(low_half, 0.0, vt)], axis=0).astype(bf16)
            s = jnp.dot(w_s_ref[c], stacked, preferred_element_type=f32) + b_s_ref[c]
            gm_sc[rows, sl] = gu[rows, sl] * s

    first_blk = step == 0
    last_step = step == n_steps - 1

    def attend(qb, carry):
        r0 = pl.multiple_of(qb * BLK, BLK)
        is_first = jnp.logical_and(first_blk, qb == 0)
        is_last = jnp.logical_and(last_step, qb == n_blk - 1)
        variant = jnp.where(is_first, 1, jnp.where(is_last, 2, 0))
        for j in range(N_KV):
            kz = jnp.concatenate([kz_sc[2 * j, pl.ds(r0, 3 * BLK), :],
                                  kz_sc[2 * j + 1, pl.ds(r0, 3 * BLK), :]], axis=0)
            vz = jnp.concatenate([vz_sc[2 * j, pl.ds(r0, 3 * BLK), :],
                                  vz_sc[2 * j + 1, pl.ds(r0, 3 * BLK), :]], axis=0)
            tiles = [j * (PAIRS // N_KV) + cl for cl in range(PAIRS // N_KV)]
            qs = jnp.concatenate(
                [q_sc[pl.ds(r0, BLK), c * LANES:(c + 1) * LANES] for c in tiles], axis=0)
            s2 = lax.dot_general(qs, kz, (((1,), (1,)), ((), ())),
                                 preferred_element_type=f32)
            e_rows = []
            inv_rows = []
            for cl, c in enumerate(tiles):
                es = []
                invs = []
                for p in range(2):
                    h = 2 * c + p
                    s = (s2[cl * BLK:(cl + 1) * BLK, p * 3 * BLK:(p + 1) * 3 * BLK]
                         + bias_sc[variant, h])
                    sink = sink_ref[h]
                    m = jnp.maximum(jnp.max(s, axis=-1, keepdims=True), sink)
                    e = jnp.exp(s - m)
                    denom = jnp.sum(e, axis=-1, keepdims=True) + jnp.exp(sink - m)
                    es.append(e.astype(bf16))
                    invs.append(1.0 / denom)
                e_rows.append(jnp.concatenate(es, axis=1))
                inv_rows.append(jnp.where(low_half, invs[0], invs[1]))
            pv = jnp.dot(jnp.concatenate(e_rows, axis=0), vz, preferred_element_type=f32)
            for cl, c in enumerate(tiles):
                att_sc[pl.ds(r0, BLK), c * LANES:(c + 1) * LANES] = (
                    pv[cl * BLK:(cl + 1) * BLK] * inv_rows[cl])
        return carry

    lax.fori_loop(0, n_blk, attend, 0, unroll=True)

    a = _row_rms(att_sc[...], a_gain_ref[...]).astype(bf16)
    g = _row_rms(gm_sc[...], g_gain_ref[...]).astype(bf16)
    y = jnp.dot(a, w_o_ref[:ATT_WIDTH, :], preferred_element_type=f32)
    y = y + jnp.dot(g, w_o_ref[ATT_WIDTH:, :], preferred_element_type=f32)
    out_ref[0] = xc_ref[0] + y


def _ffn_kernel(x_ref, norm2_ref, w_gate_ref, w_up_ref, w_down_ref, out_ref):
    f32 = jnp.float32
    x = x_ref[...]
    h = _row_rms(x, norm2_ref[...]).astype(jnp.bfloat16)
    gate = jnp.dot(h, w_gate_ref[...], preferred_element_type=f32)
    up = jnp.dot(h, w_up_ref[...], preferred_element_type=f32)
    act = (gate * (1.0 / (1.0 + jnp.exp(-gate))) * up).astype(jnp.bfloat16)
    out_ref[...] = x + jnp.dot(act, w_down_ref[...], preferred_element_type=f32)


def _resident(shape):
    zeros = (0,) * len(shape)
    return pl.BlockSpec(shape, lambda *_: zeros, pipeline_mode=pl.Buffered(1))


def _mixer(x, p):
    batch, seq, d = x.shape
    tb = MIX_TOKENS
    assert seq % tb == 0 and tb % BLK == 0 and seq // BLK >= 2 and d == D_MODEL
    r = tb // BLK
    nb = seq // BLK
    smem = pl.BlockSpec(memory_space=pltpu.SMEM)
    in_specs = [
        smem,
        smem,
        _resident((BLK, 3 * BLK)),
        pl.BlockSpec((1, tb, d), lambda b, i: (b, i, 0)),
        pl.BlockSpec((1, BLK, d), lambda b, i: (b, jnp.maximum(i * r - 1, 0), 0)),
        pl.BlockSpec((1, BLK, d), lambda b, i: (b, jnp.minimum((i + 1) * r, nb - 1), 0)),
        _resident((1, d)),
        _resident((d, IN_WIDTH)),
        _resident((1, ATT_WIDTH)),
        _resident((1, KV_WIDTH)),
        _resident((1, GM_WIDTH)),
        _resident((PAIRS, BLK, 2 * BLK)),
        _resident((PAIRS, BLK, LANES)),
        _resident((1, ATT_WIDTH)),
        _resident((1, GM_WIDTH)),
        _resident((d, d)),
    ]
    scratch = [
        pltpu.VMEM((3, N_HEADS, BLK, 3 * BLK), jnp.float32),
        pltpu.VMEM((tb, ATT_WIDTH), jnp.bfloat16),
        pltpu.VMEM((2 * N_KV, tb + 2 * BLK, LANES), jnp.bfloat16),
        pltpu.VMEM((2 * N_KV, tb + 2 * BLK, LANES), jnp.bfloat16),
        pltpu.VMEM((tb, ATT_WIDTH), jnp.float32),
        pltpu.VMEM((tb, GM_WIDTH), jnp.float32),
    ]
    return pl.pallas_call(
        _mixer_kernel,
        grid=(batch, seq // tb),
        in_specs=in_specs,
        out_specs=pl.BlockSpec((1, tb, d), lambda b, i: (b, i, 0)),
        out_shape=jax.ShapeDtypeStruct(x.shape, x.dtype),
        scratch_shapes=scratch,
        compiler_params=pltpu.CompilerParams(
            dimension_semantics=("arbitrary", "arbitrary"),
            vmem_limit_bytes=VMEM_LIMIT_BYTES),
        name="mixer",
    )(p["table"], p["sink"], p["bucket"], x, x, x, p["norm1"], p["w_in"], p["q_gain"],
      p["k_gain"], p["v_gain"], p["w_s"], p["b_s"], p["a_gain"], p["g_gain"], p["w_o"])


def _ffn(x2d, p):
    n, d = x2d.shape
    tm = FFN_TOKENS
    assert n % tm == 0
    d_ff = p["w_gate"].shape[1]
    return pl.pallas_call(
        _ffn_kernel,
        grid=(n // tm,),
        in_specs=[
            pl.BlockSpec((tm, d), lambda i: (i, 0)),
            _resident((1, d)),
            _resident((d, d_ff)),
            _resident((d, d_ff)),
            _resident((d_ff, d)),
        ],
        out_specs=pl.BlockSpec((tm, d), lambda i: (i, 0)),
        out_shape=jax.ShapeDtypeStruct(x2d.shape, x2d.dtype),
        compiler_params=pltpu.CompilerParams(
            dimension_semantics=("arbitrary",),
            vmem_limit_bytes=VMEM_LIMIT_BYTES),
        name="ffn",
    )(x2d, p["norm2"], p["w_gate"], p["w_up"], p["w_down"])


def _layer_params(rel_bias_table, norm1, w_in, q_gain, k_gain, sink, v_gain, w_s, b_s,
                  attn_out_gain, gmlp_out_gain, w_o, norm2, w_gate, w_up, w_down):
    bf16 = jnp.bfloat16
    row = lambda v: v.reshape(1, -1).astype(jnp.float32)
    w_pairs = jnp.concatenate([w_s[0::2], w_s[1::2]], axis=-1).astype(bf16)
    b_pairs = jnp.repeat(jnp.transpose(b_s.reshape(PAIRS, 2, BLK), (0, 2, 1)), HEAD_DIM, axis=-1)
    return {
        "table": rel_bias_table.astype(jnp.float32),
        "sink": sink.astype(jnp.float32),
        "bucket": jnp.asarray(_t5_bucket_table()),
        "norm1": row(norm1),
        "w_in": w_in.astype(bf16),
        "q_gain": row(jnp.tile(q_gain, N_HEADS)),
        "k_gain": row(jnp.tile(k_gain, N_KV)),
        "v_gain": row(v_gain),
        "w_s": w_pairs,
        "b_s": b_pairs.astype(jnp.float32),
        "a_gain": row(attn_out_gain),
        "g_gain": row(gmlp_out_gain),
        "w_o": w_o.astype(bf16),
        "norm2": row(norm2),
        "w_gate": w_gate.astype(bf16),
        "w_up": w_up.astype(bf16),
        "w_down": w_down.astype(bf16),
    }


def kernel(x_prompt, x_sample, rel_bias_table, norm1, w_in, q_gain, k_gain, sink, v_gain, w_s, b_s,
           attn_out_gain, gmlp_out_gain, w_o, norm2, w_gate, w_up, w_down):
    layers = [
        _layer_params(rel_bias_table, norm1[l], w_in[l], q_gain[l], k_gain[l], sink[l], v_gain[l],
                      w_s[l], b_s[l], attn_out_gain[l], gmlp_out_gain[l], w_o[l], norm2[l],
                      w_gate[l], w_up[l], w_down[l])
        for l in range(norm1.shape[0])]
    outs = []
    for x in (x_prompt, x_sample):
        for p in layers:
            x = _mixer(x, p)
            x = _ffn(x.reshape(-1, D_MODEL), p).reshape(x.shape)
        outs.append(x)
    return tuple(outs)
```

```python
import functools
import math

import jax
import jax.numpy as jnp
import numpy as np
from jax import lax
from jax.experimental import pallas as pl
from jax.experimental.pallas import tpu as pltpu

D_MODEL = 1024
HEAD_DIM = 64
N_HEADS = 8
N_KV = 2
ATT_WIDTH = N_HEADS * HEAD_DIM
KV_WIDTH = N_KV * HEAD_DIM
GM_HEADS = 8
GM_WIDTH = GM_HEADS * HEAD_DIM
IN_WIDTH = ATT_WIDTH + 2 * KV_WIDTH + 2 * GM_WIDTH
BLK = 128
N_BUCKETS = 32
MAX_DIST = 128
EPS = 1e-6
NEG = -1e30

LANES = 128
PAIRS = ATT_WIDTH // LANES
MIX_TOKENS = 512
FFN_TOKENS = 256
VMEM_LIMIT_BYTES = 56 * 1024 * 1024

_Q_OFF = 0
_KV_OFF = ATT_WIDTH
_G_OFF = ATT_WIDTH + 2 * KV_WIDTH


def _t5_bucket_table():
    rel = (np.arange(3 * BLK) - BLK)[None, :] - np.arange(BLK)[:, None]
    half = N_BUCKETS // 2
    max_exact = half // 2
    ret = (rel > 0).astype(np.int32) * half
    n = np.abs(rel)
    large = max_exact + (np.log(np.maximum(n, 1).astype(np.float32) / max_exact)
                         / np.log(MAX_DIST / max_exact) * (half - max_exact)).astype(np.int32)
    large = np.minimum(large, half - 1)
    bucket = (ret + np.where(n < max_exact, n, large)).astype(np.int32)
    return np.where(np.abs(rel) <= BLK, bucket, -1).astype(np.int32)


def _row_rms(x, gain):
    ms = jnp.mean(x * x, axis=-1, keepdims=True)
    return x * lax.rsqrt(ms + EPS) * gain


def _head_rms(x, gain, low_half):
    x2 = x * x
    s_lo = jnp.sum(jnp.where(low_half, x2, 0.0), axis=-1, keepdims=True)
    s_hi = jnp.sum(jnp.where(low_half, 0.0, x2), axis=-1, keepdims=True)
    ms = jnp.where(low_half, s_lo, s_hi) * (1.0 / HEAD_DIM)
    return x * lax.rsqrt(ms + EPS) * gain


def _build_bias(table_ref, bucket_ref, bias_sc):
    f32 = jnp.float32
    neg = jnp.full((BLK, BLK), NEG, f32)
    for h in range(N_HEADS):
        for part in range(3):
            bk = bucket_ref[:, part * BLK:(part + 1) * BLK]

            def body(b, acc):
                return jnp.where(bk == b, table_ref[b, h], acc)

            acc = lax.fori_loop(0, N_BUCKETS, body, neg)
            sl = slice(part * BLK, (part + 1) * BLK)
            bias_sc[0, h, :, sl] = acc
            bias_sc[1, h, :, sl] = neg if part == 0 else acc
            bias_sc[2, h, :, sl] = neg if part == 2 else acc


def _front_pieces(xc_ref, xp_ref, xn_ref, norm1_ref, w_in_ref, q_gain_ref, k_gain_ref,
                  v_gain_ref, w_s_ref, b_s_ref, g_gain_ref, gu_sc, gv_sc, q_sc, kz_sc, vz_sc, g_sc):
    f32 = jnp.float32
    bf16 = jnp.bfloat16
    n_blk = xc_ref.shape[1] // BLK
    low_half = lax.broadcasted_iota(jnp.int32, (1, LANES), 1) < HEAD_DIM
    wide = 2 * LANES
    state = {}

    def project(h, off):
        return jnp.dot(h, w_in_ref[:, off:off + wide], preferred_element_type=f32)

    def keys_values():
        n1 = norm1_ref[...]
        hc = _row_rms(xc_ref[0], n1).astype(bf16)
        hp = _row_rms(xp_ref[0], n1).astype(bf16)
        hn = _row_rms(xn_ref[0], n1).astype(bf16)
        state["hc"] = hc
        kv = project(jnp.concatenate([hp, hc, hn], axis=0), _KV_OFF)
        kn = _head_rms(kv[:, :KV_WIDTH], k_gain_ref[...], low_half)
        vv = kv[:, KV_WIDTH:]
        kn_sw = pltpu.roll(kn, HEAD_DIM, axis=1)
        vv_sw = pltpu.roll(vv, HEAD_DIM, axis=1)
        zero = jnp.zeros_like(kn)
        for idx, (src_k, src_v, keep_low) in enumerate(
                ((kn, vv, True), (kn_sw, vv_sw, False), (kn_sw, vv_sw, True), (kn, vv, False))):
            keep = low_half if keep_low else jnp.logical_not(low_half)
            kz_sc[idx] = jnp.where(keep, src_k, zero).astype(bf16)
            vz_sc[idx] = jnp.where(keep, src_v, zero).astype(bf16)

    def queries(tile):
        def run():
            qf = project(state["hc"], _Q_OFF + tile * wide)
            for cl in range(wide // LANES):
                sl = slice(tile * wide + cl * LANES, tile * wide + (cl + 1) * LANES)
                qn = _head_rms(qf[:, cl * LANES:(cl + 1) * LANES], q_gain_ref[:, sl], low_half)
                q_sc[:, sl] = (qn * (1.0 / math.sqrt(HEAD_DIM))).astype(bf16)
        return run

    def gelu_tile(tile):
        def run():
            gf = jax.nn.gelu(project(state["hc"], _G_OFF + tile * wide))
            dst, col = (gu_sc, tile * wide) if tile * wide < GM_WIDTH else (
                gv_sc, tile * wide - GM_WIDTH)
            dst[:, col:col + wide] = gf
        return run

    def gate_rows(r):
        def run():
            rows = slice(r * BLK, (r + 1) * BLK)
            gv = _row_rms(gv_sc[rows, :], v_gain_ref[...])
            tiles = []
            for c in range(PAIRS):
                sl = slice(c * LANES, (c + 1) * LANES)
                vt = gv[:, sl]
                stacked = jnp.concatenate(
                    [jnp.where(low_half, vt, 0.0), jnp.where(low_half, 0.0, vt)],
                    axis=0).astype(bf16)
                s = jnp.dot(w_s_ref[c], stacked, preferred_element_type=f32) + b_s_ref[c]
                tiles.append(gu_sc[rows, sl] * s)
            gm = jnp.concatenate(tiles, axis=1)
            g_sc[rows, :] = _row_rms(gm, g_gain_ref[...]).astype(bf16)
        return run

    pieces = [keys_values]
    pieces += [queries(tile) for tile in range(ATT_WIDTH // wide)]
    pieces += [gelu_tile(tile) for tile in range(2 * GM_WIDTH // wide)]
    pieces += [gate_rows(r) for r in range(n_blk)]
    return pieces


def _back_pieces(x_ref, sink_ref, a_gain_ref, w_o_ref, bias_sc, q_sc, kz_sc, vz_sc, g_sc, att_sc,
                 out_ref, seq_first, seq_last):
    f32 = jnp.float32
    bf16 = jnp.bfloat16
    n_blk = x_ref.shape[1] // BLK
    low_half = lax.broadcasted_iota(jnp.int32, (1, LANES), 1) < HEAD_DIM
    tiles_per_kv = PAIRS // N_KV

    def attend(qb, j):
        def run():
            r0 = qb * BLK
            if qb == 0:
                variant = jnp.where(seq_first, 1, 0)
            elif qb == n_blk - 1:
                variant = jnp.where(seq_last, 2, 0)
            else:
                variant = 0
            band = slice(r0, r0 + 3 * BLK)
            kz = jnp.concatenate([kz_sc[2 * j, band, :], kz_sc[2 * j + 1, band, :]], axis=0)
            vz = jnp.concatenate([vz_sc[2 * j, band, :], vz_sc[2 * j + 1, band, :]], axis=0)
            tiles = [j * tiles_per_kv + cl for cl in range(tiles_per_kv)]
            qs = jnp.concatenate(
                [q_sc[r0:r0 + BLK, c * LANES:(c + 1) * LANES] for c in tiles], axis=0)
            s2 = lax.dot_general(qs, kz, (((1,), (1,)), ((), ())),
                                 preferred_element_type=f32)
            e_rows = []
            inv_rows = []
            for cl, c in enumerate(tiles):
                es = []
                invs = []
                for p in range(2):
                    h = 2 * c + p
                    s = (s2[cl * BLK:(cl + 1) * BLK, p * 3 * BLK:(p + 1) * 3 * BLK]
                         + bias_sc[variant, h])
                    sink = sink_ref[h]
                    m = jnp.maximum(jnp.max(s, axis=-1, keepdims=True), sink)
                    e = jnp.exp(s - m)
                    denom = jnp.sum(e, axis=-1, keepdims=True) + jnp.exp(sink - m)
                    es.append(e.astype(bf16))
                    invs.append(1.0 / denom)
                e_rows.append(jnp.concatenate(es, axis=1))
                inv_rows.append(jnp.where(low_half, invs[0], invs[1]))
            pv = jnp.dot(jnp.concatenate(e_rows, axis=0), vz, preferred_element_type=f32)
            for cl, c in enumerate(tiles):
                att_sc[r0:r0 + BLK, c * LANES:(c + 1) * LANES] = (
                    pv[cl * BLK:(cl + 1) * BLK] * inv_rows[cl])
        return run

    def project_out(half):
        def run():
            rows = slice(half * 2 * BLK, (half + 1) * 2 * BLK)
            a = _row_rms(att_sc[rows, :], a_gain_ref[...]).astype(bf16)
            y = jnp.dot(a, w_o_ref[:ATT_WIDTH, :], preferred_element_type=f32)
            y = y + jnp.dot(g_sc[rows, :], w_o_ref[ATT_WIDTH:, :], preferred_element_type=f32)
            out_ref[0, rows, :] = x_ref[0, rows, :] + y
        return run

    attention = [attend(qb, j) for qb in range(n_blk) for j in range(N_KV)]
    outputs = [project_out(half) for half in range(n_blk // 2)]
    return attention, outputs


def _mixer_kernel(table_ref, sink_ref, bucket_ref, xc_ref, xp_ref, xn_ref, xl_ref, norm1_ref,
                  w_in_ref, q_gain_ref, k_gain_ref, v_gain_ref, w_s_ref, b_s_ref, a_gain_ref,
                  g_gain_ref, w_o_ref, out_ref, bias_sc, att_sc, gu_sc, gv_sc, *stage_sc,
                  blocks_per_seq):
    t = pl.program_id(0)
    set_a, set_b = stage_sc[:4], stage_sc[4:]

    @pl.when(t == 0)
    def _init():
        _build_bias(table_ref, bucket_ref, bias_sc)
        for ref in set_b:
            ref[...] = jnp.zeros(ref.shape, ref.dtype)

    lag = jnp.maximum(t - 1, 0)
    pos = lax.rem(lag, blocks_per_seq)
    seq_first = pos == 0
    seq_last = pos == blocks_per_seq - 1

    def run(write_set, read_set):
        front = _front_pieces(xc_ref, xp_ref, xn_ref, norm1_ref, w_in_ref, q_gain_ref,
                              k_gain_ref, v_gain_ref, w_s_ref, b_s_ref, g_gain_ref, gu_sc, gv_sc,
                              *write_set)
        attention, outputs = _back_pieces(xl_ref, sink_ref, a_gain_ref, w_o_ref, bias_sc,
                                          *read_set, att_sc, out_ref, seq_first, seq_last)
        per_half = len(attention) // len(outputs)
        back = []
        for half, project_out in enumerate(outputs):
            back += attention[half * per_half:(half + 1) * per_half]
            back.append(project_out)
        for i in range(max(len(front), len(back))):
            if i < len(back):
                back[i]()
            if i < len(front):
                front[i]()

    parity = lax.rem(t, 2)

    @pl.when(parity == 0)
    def _even():
        run(set_a, set_b)

    @pl.when(parity == 1)
    def _odd():
        run(set_b, set_a)


def _ffn_kernel(x_ref, norm2_ref, w_gate_ref, w_up_ref, w_down_ref, out_ref):
    f32 = jnp.float32
    x = x_ref[...]
    h = _row_rms(x, norm2_ref[...]).astype(jnp.bfloat16)
    gate = jnp.dot(h, w_gate_ref[...], preferred_element_type=f32)
    up = jnp.dot(h, w_up_ref[...], preferred_element_type=f32)
    act = (gate * (1.0 / (1.0 + jnp.exp(-gate))) * up).astype(jnp.bfloat16)
    out_ref[...] = x + jnp.dot(act, w_down_ref[...], preferred_element_type=f32)


def _resident(shape):
    zeros = (0,) * len(shape)
    return pl.BlockSpec(shape, lambda *_: zeros, pipeline_mode=pl.Buffered(1))


def _mixer(x, p):
    batch, seq, d = x.shape
    tb = MIX_TOKENS
    assert seq % tb == 0 and tb % (2 * BLK) == 0 and seq // BLK >= 2 and d == D_MODEL
    r = tb // BLK
    nb = seq // BLK
    n_i = seq // tb
    n_t = batch * n_i

    def cur(t):
        tc = jnp.minimum(t, n_t - 1)
        return lax.div(tc, n_i), lax.rem(tc, n_i)

    def lagged(t):
        tl = jnp.maximum(t - 1, 0)
        return lax.div(tl, n_i), lax.rem(tl, n_i)

    def cur_map(t):
        b, i = cur(t)
        return b, i, 0

    def prev_map(t):
        b, i = cur(t)
        return b, jnp.maximum(i * r - 1, 0), 0

    def next_map(t):
        b, i = cur(t)
        return b, jnp.minimum((i + 1) * r, nb - 1), 0

    def lag_map(t):
        b, i = lagged(t)
        return b, i, 0

    smem = pl.BlockSpec(memory_space=pltpu.SMEM)
    in_specs = [
        smem,
        smem,
        _resident((BLK, 3 * BLK)),
        pl.BlockSpec((1, tb, d), cur_map),
        pl.BlockSpec((1, BLK, d), prev_map),
        pl.BlockSpec((1, BLK, d), next_map),
        pl.BlockSpec((1, tb, d), lag_map),
        _resident((1, d)),
        _resident((d, IN_WIDTH)),
        _resident((1, ATT_WIDTH)),
        _resident((1, KV_WIDTH)),
        _resident((1, GM_WIDTH)),
        _resident((PAIRS, BLK, 2 * BLK)),
        _resident((PAIRS, BLK, LANES)),
        _resident((1, ATT_WIDTH)),
        _resident((1, GM_WIDTH)),
        _resident((d, d)),
    ]
    stage = [
        pltpu.VMEM((tb, ATT_WIDTH), jnp.bfloat16),
        pltpu.VMEM((2 * N_KV, tb + 2 * BLK, LANES), jnp.bfloat16),
        pltpu.VMEM((2 * N_KV, tb + 2 * BLK, LANES), jnp.bfloat16),
        pltpu.VMEM((tb, GM_WIDTH), jnp.bfloat16),
    ]
    scratch = [
        pltpu.VMEM((3, N_HEADS, BLK, 3 * BLK), jnp.float32),
        pltpu.VMEM((tb, ATT_WIDTH), jnp.float32),
        pltpu.VMEM((tb, GM_WIDTH), jnp.float32),
        pltpu.VMEM((tb, GM_WIDTH), jnp.float32),
    ] + stage + stage
    return pl.pallas_call(
        functools.partial(_mixer_kernel, blocks_per_seq=n_i),
        grid=(n_t + 1,),
        in_specs=in_specs,
        out_specs=pl.BlockSpec((1, tb, d), lag_map),
        out_shape=jax.ShapeDtypeStruct(x.shape, x.dtype),
        scratch_shapes=scratch,
        compiler_params=pltpu.CompilerParams(
            dimension_semantics=("arbitrary",),
            vmem_limit_bytes=VMEM_LIMIT_BYTES),
        name="mixer",
    )(p["table"], p["sink"], p["bucket"], x, x, x, x, p["norm1"], p["w_in"], p["q_gain"],
      p["k_gain"], p["v_gain"], p["w_s"], p["b_s"], p["a_gain"], p["g_gain"], p["w_o"])


def _ffn(x2d, p):
    n, d = x2d.shape
    tm = FFN_TOKENS
    assert n % tm == 0
    d_ff = p["w_gate"].shape[1]
    return pl.pallas_call(
        _ffn_kernel,
        grid=(n // tm,),
        in_specs=[
            pl.BlockSpec((tm, d), lambda i: (i, 0)),
            _resident((1, d)),
            _resident((d, d_ff)),
            _resident((d, d_ff)),
            _resident((d_ff, d)),
        ],
        out_specs=pl.BlockSpec((tm, d), lambda i: (i, 0)),
        out_shape=jax.ShapeDtypeStruct(x2d.shape, x2d.dtype),
        compiler_params=pltpu.CompilerParams(
            dimension_semantics=("arbitrary",),
            vmem_limit_bytes=VMEM_LIMIT_BYTES),
        name="ffn",
    )(x2d, p["norm2"], p["w_gate"], p["w_up"], p["w_down"])


def _layer_params(rel_bias_table, norm1, w_in, q_gain, k_gain, sink, v_gain, w_s, b_s,
                  attn_out_gain, gmlp_out_gain, w_o, norm2, w_gate, w_up, w_down):
    bf16 = jnp.bfloat16
    row = lambda v: v.reshape(1, -1).astype(jnp.float32)
    w_pairs = jnp.concatenate([w_s[0::2], w_s[1::2]], axis=-1).astype(bf16)
    b_pairs = jnp.repeat(jnp.transpose(b_s.reshape(PAIRS, 2, BLK), (0, 2, 1)), HEAD_DIM, axis=-1)
    return {
        "table": rel_bias_table.astype(jnp.float32),
        "sink": sink.astype(jnp.float32),
        "bucket": jnp.asarray(_t5_bucket_table()),
        "norm1": row(norm1),
        "w_in": w_in.astype(bf16),
        "q_gain": row(jnp.tile(q_gain, N_HEADS)),
        "k_gain": row(jnp.tile(k_gain, N_KV)),
        "v_gain": row(v_gain),
        "w_s": w_pairs,
        "b_s": b_pairs.astype(jnp.float32),
        "a_gain": row(attn_out_gain),
        "g_gain": row(gmlp_out_gain),
        "w_o": w_o.astype(bf16),
        "norm2": row(norm2),
        "w_gate": w_gate.astype(bf16),
        "w_up": w_up.astype(bf16),
        "w_down": w_down.astype(bf16),
    }


def kernel(x_prompt, x_sample, rel_bias_table, norm1, w_in, q_gain, k_gain, sink, v_gain, w_s, b_s,
           attn_out_gain, gmlp_out_gain, w_o, norm2, w_gate, w_up, w_down):
    layers = [
        _layer_params(rel_bias_table, norm1[l], w_in[l], q_gain[l], k_gain[l], sink[l], v_gain[l],
                      w_s[l], b_s[l], attn_out_gain[l], gmlp_out_gain[l], w_o[l], norm2[l],
                      w_gate[l], w_up[l], w_down[l])
        for l in range(norm1.shape[0])]
    outs = []
    for x in (x_prompt, x_sample):
        for p in layers:
            x = _mixer(x, p)
            x = _ffn(x.reshape(-1, D_MODEL), p).reshape(x.shape)
        outs.append(x)
    return tuple(outs)
```

```python
import functools
import math

import jax
import jax.numpy as jnp
import numpy as np
from jax import lax
from jax.experimental import pallas as pl
from jax.experimental.pallas import tpu as pltpu

D_MODEL = 1024
HEAD_DIM = 64
N_HEADS = 8
N_KV = 2
ATT_WIDTH = N_HEADS * HEAD_DIM
KV_WIDTH = N_KV * HEAD_DIM
GM_HEADS = 8
GM_WIDTH = GM_HEADS * HEAD_DIM
IN_WIDTH = ATT_WIDTH + 2 * KV_WIDTH + 2 * GM_WIDTH
BLK = 128
N_BUCKETS = 32
MAX_DIST = 128
EPS = 1e-6
NEG = -1e30
LOG2_E = math.log2(math.e)

LANES = 128
PAIRS = ATT_WIDTH // LANES
MIX_TOKENS = 512
FFN_TOKENS = 512
VMEM_LIMIT_BYTES = 56 * 1024 * 1024

_Q_OFF = 0
_KV_OFF = ATT_WIDTH
_G_OFF = ATT_WIDTH + 2 * KV_WIDTH


def _t5_bucket_table():
    rel = (np.arange(3 * BLK) - BLK)[None, :] - np.arange(BLK)[:, None]
    half = N_BUCKETS // 2
    max_exact = half // 2
    ret = (rel > 0).astype(np.int32) * half
    n = np.abs(rel)
    large = max_exact + (np.log(np.maximum(n, 1).astype(np.float32) / max_exact)
                         / np.log(MAX_DIST / max_exact) * (half - max_exact)).astype(np.int32)
    large = np.minimum(large, half - 1)
    bucket = (ret + np.where(n < max_exact, n, large)).astype(np.int32)
    return np.where(np.abs(rel) <= BLK, bucket, -1).astype(np.int32)


def _row_rms(x, gain):
    ms = jnp.mean(x * x, axis=-1, keepdims=True)
    return x * lax.rsqrt(ms + EPS) * gain


def _head_rms(x, gain, low_half):
    x2 = x * x
    s_lo = jnp.sum(jnp.where(low_half, x2, 0.0), axis=-1, keepdims=True)
    s_hi = jnp.sum(jnp.where(low_half, 0.0, x2), axis=-1, keepdims=True)
    ms = jnp.where(low_half, s_lo, s_hi) * (1.0 / HEAD_DIM)
    return x * lax.rsqrt(ms + EPS) * gain


def _build_bias(table_ref, bucket_ref, bias_sc):
    f32 = jnp.float32
    neg = jnp.full((BLK, BLK), NEG, f32)
    for h in range(N_HEADS):
        for part in range(3):
            bk = bucket_ref[:, part * BLK:(part + 1) * BLK]

            def body(b, acc):
                return jnp.where(bk == b, table_ref[b, h] * LOG2_E, acc)

            acc = lax.fori_loop(0, N_BUCKETS, body, neg)
            sl = slice(part * BLK, (part + 1) * BLK)
            bias_sc[0, h, :, sl] = acc
            bias_sc[1, h, :, sl] = neg if part == 0 else acc
            bias_sc[2, h, :, sl] = neg if part == 2 else acc


def _front_pieces(xc_ref, norm1_ref, w_in_ref, q_gain_ref, k_gain_ref, v_gain_ref, w_s_ref,
                  b_s_ref, g_gain_ref, gu_sc, gv_sc, kz_prev, vz_prev, q_sc, kz_sc, vz_sc, g_sc):
    f32 = jnp.float32
    bf16 = jnp.bfloat16
    tb = xc_ref.shape[1]
    n_blk = tb // BLK
    low_half = lax.broadcasted_iota(jnp.int32, (1, LANES), 1) < HEAD_DIM
    wide = 2 * LANES
    state = {}

    def project(h, off):
        return jnp.dot(h, w_in_ref[:, off:off + wide], preferred_element_type=f32)

    def keys_values():
        hc = _row_rms(xc_ref[0], norm1_ref[...]).astype(bf16)
        state["hc"] = hc
        kv = project(hc, _KV_OFF)
        kn = _head_rms(kv[:, :KV_WIDTH], k_gain_ref[...], low_half)
        vv = kv[:, KV_WIDTH:]
        kn_sw = pltpu.roll(kn, HEAD_DIM, axis=1)
        vv_sw = pltpu.roll(vv, HEAD_DIM, axis=1)
        zero = jnp.zeros_like(kn)
        tail = slice(tb, tb + BLK)
        for idx, (src_k, src_v, keep_low) in enumerate(
                ((kn, vv, True), (kn_sw, vv_sw, False), (kn_sw, vv_sw, True), (kn, vv, False))):
            keep = low_half if keep_low else jnp.logical_not(low_half)
            for src, dst, dst_prev in ((src_k, kz_sc, kz_prev), (src_v, vz_sc, vz_prev)):
                laid = jnp.where(keep, src, zero).astype(bf16)
                dst[idx, BLK:BLK + tb, :] = laid
                dst_prev[idx, BLK + tb:, :] = laid[:BLK]
                dst[idx, :BLK, :] = dst_prev[idx, tail, :]

    def queries(tile):
        def run():
            qf = project(state["hc"], _Q_OFF + tile * wide)
            for cl in range(wide // LANES):
                sl = slice(tile * wide + cl * LANES, tile * wide + (cl + 1) * LANES)
                qn = _head_rms(qf[:, cl * LANES:(cl + 1) * LANES], q_gain_ref[:, sl], low_half)
                q_sc[:, sl] = (qn * (LOG2_E / math.sqrt(HEAD_DIM))).astype(bf16)
        return run

    def gelu_tile(tile):
        def run():
            gf = jax.nn.gelu(project(state["hc"], _G_OFF + tile * wide))
            dst, col = (gu_sc, tile * wide) if tile * wide < GM_WIDTH else (
                gv_sc, tile * wide - GM_WIDTH)
            dst[:, col:col + wide] = gf
        return run

    def gate_rows(r):
        def run():
            rows = slice(r * BLK, (r + 1) * BLK)
            gv = _row_rms(gv_sc[rows, :], v_gain_ref[...])
            tiles = []
            for c in range(PAIRS):
                sl = slice(c * LANES, (c + 1) * LANES)
                vt = gv[:, sl]
                stacked = jnp.concatenate(
                    [jnp.where(low_half, vt, 0.0), jnp.where(low_half, 0.0, vt)],
                    axis=0).astype(bf16)
                s = jnp.dot(w_s_ref[c], stacked, preferred_element_type=f32) + b_s_ref[c]
                tiles.append(gu_sc[rows, sl] * s)
            gm = jnp.concatenate(tiles, axis=1)
            g_sc[rows, :] = _row_rms(gm, g_gain_ref[...]).astype(bf16)
        return run

    pieces = [keys_values]
    pieces += [queries(tile) for tile in range(ATT_WIDTH // wide)]
    pieces += [gelu_tile(tile) for tile in range(2 * GM_WIDTH // wide)]
    pieces += [gate_rows(r) for r in range(n_blk)]
    return pieces


def _back_pieces(x_ref, sink_ref, a_gain_ref, w_o_ref, bias_sc, q_sc, kz_sc, vz_sc, g_sc, att_sc,
                 out_ref, seq_first, seq_last):
    f32 = jnp.float32
    bf16 = jnp.bfloat16
    n_blk = x_ref.shape[1] // BLK
    low_half = lax.broadcasted_iota(jnp.int32, (1, LANES), 1) < HEAD_DIM
    tiles_per_kv = PAIRS // N_KV

    def attend(qb, j):
        def run():
            r0 = qb * BLK
            if qb == 0:
                variant = jnp.where(seq_first, 1, 0)
            elif qb == n_blk - 1:
                variant = jnp.where(seq_last, 2, 0)
            else:
                variant = 0
            band = slice(r0, r0 + 3 * BLK)
            kz = jnp.concatenate([kz_sc[2 * j, band, :], kz_sc[2 * j + 1, band, :]], axis=0)
            vz = jnp.concatenate([vz_sc[2 * j, band, :], vz_sc[2 * j + 1, band, :]], axis=0)
            tiles = [j * tiles_per_kv + cl for cl in range(tiles_per_kv)]
            qs = jnp.concatenate(
                [q_sc[r0:r0 + BLK, c * LANES:(c + 1) * LANES] for c in tiles], axis=0)
            s2 = lax.dot_general(qs, kz, (((1,), (1,)), ((), ())),
                                 preferred_element_type=f32)
            e_rows = []
            inv_rows = []
            for cl, c in enumerate(tiles):
                es = []
                invs = []
                for p in range(2):
                    h = 2 * c + p
                    s = (s2[cl * BLK:(cl + 1) * BLK, p * 3 * BLK:(p + 1) * 3 * BLK]
                         + bias_sc[variant, h])
                    m = jnp.max(s, axis=-1, keepdims=True)
                    e = jnp.exp2(s - m)
                    denom = jnp.sum(e, axis=-1, keepdims=True) + jnp.exp2(
                        sink_ref[h] * LOG2_E - m)
                    es.append(e.astype(bf16))
                    invs.append(1.0 / denom)
                e_rows.append(jnp.concatenate(es, axis=1))
                inv_rows.append(jnp.where(low_half, invs[0], invs[1]))
            pv = jnp.dot(jnp.concatenate(e_rows, axis=0), vz, preferred_element_type=f32)
            for cl, c in enumerate(tiles):
                att_sc[r0:r0 + BLK, c * LANES:(c + 1) * LANES] = (
                    pv[cl * BLK:(cl + 1) * BLK] * inv_rows[cl])
        return run

    def project_out(half):
        def run():
            rows = slice(half * 2 * BLK, (half + 1) * 2 * BLK)
            a = _row_rms(att_sc[rows, :], a_gain_ref[...]).astype(bf16)
            y = jnp.dot(a, w_o_ref[:ATT_WIDTH, :], preferred_element_type=f32)
            y = y + jnp.dot(g_sc[rows, :], w_o_ref[ATT_WIDTH:, :], preferred_element_type=f32)
            out_ref[0, rows, :] = x_ref[0, rows, :] + y
        return run

    attention = [attend(qb, j) for qb in range(n_blk) for j in range(N_KV)]
    outputs = [project_out(half) for half in range(n_blk // 2)]
    return attention, outputs


def _mixer_kernel(table_ref, sink_ref, bucket_ref, xc_ref, xl_ref, norm1_ref,
                  w_in_ref, q_gain_ref, k_gain_ref, v_gain_ref, w_s_ref, b_s_ref, a_gain_ref,
                  g_gain_ref, w_o_ref, out_ref, bias_sc, att_sc, gu_sc, gv_sc, *stage_sc,
                  blocks_per_seq):
    t = pl.program_id(0)
    set_a, set_b = stage_sc[:4], stage_sc[4:]

    @pl.when(t == 0)
    def _init():
        _build_bias(table_ref, bucket_ref, bias_sc)
        for ref in set_b:
            ref[...] = jnp.zeros(ref.shape, ref.dtype)

    lag = jnp.maximum(t - 1, 0)
    pos = lax.rem(lag, blocks_per_seq)
    seq_first = pos == 0
    seq_last = pos == blocks_per_seq - 1

    def run(write_set, read_set):
        front = _front_pieces(xc_ref, norm1_ref, w_in_ref, q_gain_ref, k_gain_ref, v_gain_ref,
                              w_s_ref, b_s_ref, g_gain_ref, gu_sc, gv_sc, read_set[1],
                              read_set[2], *write_set)
        attention, outputs = _back_pieces(xl_ref, sink_ref, a_gain_ref, w_o_ref, bias_sc,
                                          *read_set, att_sc, out_ref, seq_first, seq_last)
        per_half = len(attention) // len(outputs)
        back = []
        for half, project_out in enumerate(outputs):
            back += attention[half * per_half:(half + 1) * per_half]
            back.append(project_out)
        for i in range(max(len(front), len(back))):
            if i < len(back):
                back[i]()
            if i < len(front):
                front[i]()

    parity = lax.rem(t, 2)

    @pl.when(parity == 0)
    def _even():
        run(set_a, set_b)

    @pl.when(parity == 1)
    def _odd():
        run(set_b, set_a)


def _ffn_kernel(x_ref, norm2_ref, w_gate_ref, w_up_ref, w_down_ref, out_ref):
    f32 = jnp.float32
    x = x_ref[...]
    h = _row_rms(x, norm2_ref[...]).astype(jnp.bfloat16)
    gate = jnp.dot(h, w_gate_ref[...], preferred_element_type=f32)
    up = jnp.dot(h, w_up_ref[...], preferred_element_type=f32)
    act = (gate * (1.0 / (1.0 + jnp.exp(-gate))) * up).astype(jnp.bfloat16)
    out_ref[...] = x + jnp.dot(act, w_down_ref[...], preferred_element_type=f32)


def _resident(shape):
    zeros = (0,) * len(shape)
    return pl.BlockSpec(shape, lambda *_: zeros, pipeline_mode=pl.Buffered(1))


def _mixer(x, p):
    batch, seq, d = x.shape
    tb = MIX_TOKENS
    assert seq % tb == 0 and tb % (2 * BLK) == 0 and seq // BLK >= 2 and d == D_MODEL
    n_i = seq // tb
    n_t = batch * n_i

    def cur(t):
        tc = jnp.minimum(t, n_t - 1)
        return lax.div(tc, n_i), lax.rem(tc, n_i)

    def lagged(t):
        tl = jnp.maximum(t - 1, 0)
        return lax.div(tl, n_i), lax.rem(tl, n_i)

    def cur_map(t):
        b, i = cur(t)
        return b, i, 0

    def lag_map(t):
        b, i = lagged(t)
        return b, i, 0

    smem = pl.BlockSpec(memory_space=pltpu.SMEM)
    in_specs = [
        smem,
        smem,
        _resident((BLK, 3 * BLK)),
        pl.BlockSpec((1, tb, d), cur_map),
        pl.BlockSpec((1, tb, d), lag_map),
        _resident((1, d)),
        _resident((d, IN_WIDTH)),
        _resident((1, ATT_WIDTH)),
        _resident((1, KV_WIDTH)),
        _resident((1, GM_WIDTH)),
        _resident((PAIRS, BLK, 2 * BLK)),
        _resident((PAIRS, BLK, LANES)),
        _resident((1, ATT_WIDTH)),
        _resident((1, GM_WIDTH)),
        _resident((d, d)),
    ]
    stage = [
        pltpu.VMEM((tb, ATT_WIDTH), jnp.bfloat16),
        pltpu.VMEM((2 * N_KV, tb + 2 * BLK, LANES), jnp.bfloat16),
        pltpu.VMEM((2 * N_KV, tb + 2 * BLK, LANES), jnp.bfloat16),
        pltpu.VMEM((tb, GM_WIDTH), jnp.bfloat16),
    ]
    scratch = [
        pltpu.VMEM((3, N_HEADS, BLK, 3 * BLK), jnp.float32),
        pltpu.VMEM((tb, ATT_WIDTH), jnp.float32),
        pltpu.VMEM((tb, GM_WIDTH), jnp.float32),
        pltpu.VMEM((tb, GM_WIDTH), jnp.float32),
    ] + stage + stage
    return pl.pallas_call(
        functools.partial(_mixer_kernel, blocks_per_seq=n_i),
        grid=(n_t + 1,),
        in_specs=in_specs,
        out_specs=pl.BlockSpec((1, tb, d), lag_map),
        out_shape=jax.ShapeDtypeStruct(x.shape, x.dtype),
        scratch_shapes=scratch,
        compiler_params=pltpu.CompilerParams(
            dimension_semantics=("arbitrary",),
            vmem_limit_bytes=VMEM_LIMIT_BYTES),
        name="mixer",
    )(p["table"], p["sink"], p["bucket"], x, x, p["norm1"], p["w_in"], p["q_gain"],
      p["k_gain"], p["v_gain"], p["w_s"], p["b_s"], p["a_gain"], p["g_gain"], p["w_o"])


def _ffn(x2d, p):
    n, d = x2d.shape
    tm = FFN_TOKENS
    assert n % tm == 0
    d_ff = p["w_gate"].shape[1]
    return pl.pallas_call(
        _ffn_kernel,
        grid=(n // tm,),
        in_specs=[
            pl.BlockSpec((tm, d), lambda i: (i, 0)),
            _resident((1, d)),
            _resident((d, d_ff)),
            _resident((d, d_ff)),
            _resident((d_ff, d)),
        ],
        out_specs=pl.BlockSpec((tm, d), lambda i: (i, 0)),
        out_shape=jax.ShapeDtypeStruct(x2d.shape, x2d.dtype),
        compiler_params=pltpu.CompilerParams(
            dimension_semantics=("arbitrary",),
            vmem_limit_bytes=VMEM_LIMIT_BYTES),
        name="ffn",
    )(x2d, p["norm2"], p["w_gate"], p["w_up"], p["w_down"])


def _layer_params(rel_bias_table, norm1, w_in, q_gain, k_gain, sink, v_gain, w_s, b_s,
                  attn_out_gain, gmlp_out_gain, w_o, norm2, w_gate, w_up, w_down):
    bf16 = jnp.bfloat16
    row = lambda v: v.reshape(1, -1).astype(jnp.float32)
    w_pairs = jnp.concatenate([w_s[0::2], w_s[1::2]], axis=-1).astype(bf16)
    b_pairs = jnp.repeat(jnp.transpose(b_s.reshape(PAIRS, 2, BLK), (0, 2, 1)), HEAD_DIM, axis=-1)
    return {
        "table": rel_bias_table.astype(jnp.float32),
        "sink": sink.astype(jnp.float32),
        "bucket": jnp.asarray(_t5_bucket_table()),
        "norm1": row(norm1),
        "w_in": w_in.astype(bf16),
        "q_gain": row(jnp.tile(q_gain, N_HEADS)),
        "k_gain": row(jnp.tile(k_gain, N_KV)),
        "v_gain": row(v_gain),
        "w_s": w_pairs,
        "b_s": b_pairs.astype(jnp.float32),
        "a_gain": row(attn_out_gain),
        "g_gain": row(gmlp_out_gain),
        "w_o": w_o.astype(bf16),
        "norm2": row(norm2),
        "w_gate": w_gate.astype(bf16),
        "w_up": w_up.astype(bf16),
        "w_down": w_down.astype(bf16),
    }


def kernel(x_prompt, x_sample, rel_bias_table, norm1, w_in, q_gain, k_gain, sink, v_gain, w_s, b_s,
           attn_out_gain, gmlp_out_gain, w_o, norm2, w_gate, w_up, w_down):
    layers = [
        _layer_params(rel_bias_table, norm1[l], w_in[l], q_gain[l], k_gain[l], sink[l], v_gain[l],
                      w_s[l], b_s[l], attn_out_gain[l], gmlp_out_gain[l], w_o[l], norm2[l],
                      w_gate[l], w_up[l], w_down[l])
        for l in range(norm1.shape[0])]
    outs = []
    for x in (x_prompt, x_sample):
        for p in layers:
            x = _mixer(x, p)
            x = _ffn(x.reshape(-1, D_MODEL), p).reshape(x.shape)
        outs.append(x)
    return tuple(outs)
```

```python
import functools
import math

import jax
import jax.numpy as jnp
import numpy as np
from jax import lax
from jax.experimental import pallas as pl
from jax.experimental.pallas import tpu as pltpu

D_MODEL = 1024
HEAD_DIM = 64
N_HEADS = 8
N_KV = 2
ATT_WIDTH = N_HEADS * HEAD_DIM
KV_WIDTH = N_KV * HEAD_DIM
GM_HEADS = 8
GM_WIDTH = GM_HEADS * HEAD_DIM
IN_WIDTH = ATT_WIDTH + 2 * KV_WIDTH + 2 * GM_WIDTH
BLK = 128
N_BUCKETS = 32
MAX_DIST = 128
EPS = 1e-6
NEG = -1e30
LOG2_E = math.log2(math.e)

LANES = 128
PAIRS = ATT_WIDTH // LANES
MIX_TOKENS = 512
FFN_TOKENS = 512
VMEM_LIMIT_BYTES = 56 * 1024 * 1024

_Q_OFF = 0
_KV_OFF = ATT_WIDTH
_G_OFF = ATT_WIDTH + 2 * KV_WIDTH


def _t5_bucket_table():
    rel = (np.arange(3 * BLK) - BLK)[None, :] - np.arange(BLK)[:, None]
    half = N_BUCKETS // 2
    max_exact = half // 2
    ret = (rel > 0).astype(np.int32) * half
    n = np.abs(rel)
    large = max_exact + (np.log(np.maximum(n, 1).astype(np.float32) / max_exact)
                         / np.log(MAX_DIST / max_exact) * (half - max_exact)).astype(np.int32)
    large = np.minimum(large, half - 1)
    bucket = (ret + np.where(n < max_exact, n, large)).astype(np.int32)
    return np.where(np.abs(rel) <= BLK, bucket, -1).astype(np.int32)


def _key_set_indicator():
    key_is_low = np.arange(6 * BLK)[:, None] < 3 * BLK
    lane_is_low = np.arange(LANES)[None, :] < HEAD_DIM
    return (key_is_low == lane_is_low).astype(np.float32)


def _row_rms(x, gain):
    ms = jnp.mean(x * x, axis=-1, keepdims=True)
    return x * lax.rsqrt(ms + EPS) * gain


def _head_rms(x, gain, low_half):
    x2 = x * x
    s_lo = jnp.sum(jnp.where(low_half, x2, 0.0), axis=-1, keepdims=True)
    s_hi = jnp.sum(jnp.where(low_half, 0.0, x2), axis=-1, keepdims=True)
    r_lo = lax.rsqrt(s_lo * (1.0 / HEAD_DIM) + EPS)
    r_hi = lax.rsqrt(s_hi * (1.0 / HEAD_DIM) + EPS)
    return x * jnp.where(low_half, r_lo, r_hi) * gain


def _gelu_tanh(x):
    a = -2.0 * math.sqrt(2.0 / math.pi) * LOG2_E
    w = x * (x * x * (a * 0.044715) + a)
    return x * (1.0 / (1.0 + jnp.exp2(w)))


def _build_bias(table_ref, bucket_ref, bias_sc):
    f32 = jnp.float32
    neg = jnp.full((BLK, BLK), NEG, f32)
    for h in range(N_HEADS):
        for part in range(3):
            bk = bucket_ref[:, part * BLK:(part + 1) * BLK]

            def body(b, acc):
                return jnp.where(bk == b, table_ref[b, h] * LOG2_E, acc)

            acc = lax.fori_loop(0, N_BUCKETS, body, neg)
            sl = slice(part * BLK, (part + 1) * BLK)
            bias_sc[0, h, :, sl] = acc
            bias_sc[1, h, :, sl] = neg if part == 0 else acc
            bias_sc[2, h, :, sl] = neg if part == 2 else acc


def _front_pieces(xc_ref, norm1_ref, w_in_ref, q_gain_ref, k_gain_ref, v_gain_ref, w_s_ref,
                  b_s_ref, g_gain_ref, gu_sc, gv_sc, kz_prev, vz_prev, q_sc, kz_sc, vz_sc, g_sc):
    f32 = jnp.float32
    bf16 = jnp.bfloat16
    tb = xc_ref.shape[1]
    n_blk = tb // BLK
    low_half = lax.broadcasted_iota(jnp.int32, (1, LANES), 1) < HEAD_DIM
    wide = 2 * LANES
    state = {}

    def project(h, off):
        return jnp.dot(h, w_in_ref[:, off:off + wide], preferred_element_type=f32)

    def keys_values():
        hc = _row_rms(xc_ref[0], norm1_ref[...]).astype(bf16)
        state["hc"] = hc
        kv = project(hc, _KV_OFF)
        kn = _head_rms(kv[:, :KV_WIDTH], k_gain_ref[...], low_half)
        vv = kv[:, KV_WIDTH:]
        kn_sw = pltpu.roll(kn, HEAD_DIM, axis=1)
        vv_sw = pltpu.roll(vv, HEAD_DIM, axis=1)
        zero = jnp.zeros_like(kn)
        tail = slice(tb, tb + BLK)
        for idx, (src_k, src_v, keep_low) in enumerate(
                ((kn, vv, True), (kn_sw, vv_sw, False), (kn_sw, vv_sw, True), (kn, vv, False))):
            keep = low_half if keep_low else jnp.logical_not(low_half)
            for src, dst, dst_prev in ((src_k, kz_sc, kz_prev), (src_v, vz_sc, vz_prev)):
                laid = jnp.where(keep, src, zero).astype(bf16)
                dst[idx, BLK:BLK + tb, :] = laid
                dst_prev[idx, BLK + tb:, :] = laid[:BLK]
                dst[idx, :BLK, :] = dst_prev[idx, tail, :]

    def queries(tile):
        def run():
            qf = project(state["hc"], _Q_OFF + tile * wide)
            for cl in range(wide // LANES):
                sl = slice(tile * wide + cl * LANES, tile * wide + (cl + 1) * LANES)
                gain = q_gain_ref[:, sl] * (LOG2_E / math.sqrt(HEAD_DIM))
                q_sc[:, sl] = _head_rms(qf[:, cl * LANES:(cl + 1) * LANES], gain,
                                        low_half).astype(bf16)
        return run

    def gelu_tile(tile):
        def run():
            gf = _gelu_tanh(project(state["hc"], _G_OFF + tile * wide))
            dst, col = (gu_sc, tile * wide) if tile * wide < GM_WIDTH else (
                gv_sc, tile * wide - GM_WIDTH)
            dst[:, col:col + wide] = gf
        return run

    def gate_rows(r):
        def run():
            rows = slice(r * BLK, (r + 1) * BLK)
            gv = _row_rms(gv_sc[rows, :], v_gain_ref[...])
            tiles = []
            for c in range(PAIRS):
                sl = slice(c * LANES, (c + 1) * LANES)
                vt = gv[:, sl]
                stacked = jnp.concatenate(
                    [jnp.where(low_half, vt, 0.0), jnp.where(low_half, 0.0, vt)],
                    axis=0).astype(bf16)
                s = jnp.dot(w_s_ref[c], stacked, preferred_element_type=f32) + b_s_ref[c]
                tiles.append(gu_sc[rows, sl] * s)
            gm = jnp.concatenate(tiles, axis=1)
            g_sc[rows, :] = _row_rms(gm, g_gain_ref[...]).astype(bf16)
        return run

    pieces = [keys_values]
    pieces += [queries(tile) for tile in range(ATT_WIDTH // wide)]
    pieces += [gelu_tile(tile) for tile in range(2 * GM_WIDTH // wide)]
    pieces += [gate_rows(r) for r in range(n_blk)]
    return pieces


def _back_pieces(x_ref, sink_ref, ind_ref, a_gain_ref, w_o_ref, bias_sc, q_sc, kz_sc, vz_sc, g_sc,
                 att_sc, out_ref, seq_first, seq_last):
    f32 = jnp.float32
    bf16 = jnp.bfloat16
    n_blk = x_ref.shape[1] // BLK
    low_half = lax.broadcasted_iota(jnp.int32, (1, LANES), 1) < HEAD_DIM
    tiles_per_kv = PAIRS // N_KV

    def attend(qb, j):
        def run():
            r0 = qb * BLK
            if qb == 0:
                variant = jnp.where(seq_first, 1, 0)
            elif qb == n_blk - 1:
                variant = jnp.where(seq_last, 2, 0)
            else:
                variant = 0
            band = slice(r0, r0 + 3 * BLK)
            kz = jnp.concatenate([kz_sc[2 * j, band, :], kz_sc[2 * j + 1, band, :]], axis=0)
            vz = jnp.concatenate([vz_sc[2 * j, band, :], vz_sc[2 * j + 1, band, :]], axis=0)
            tiles = [j * tiles_per_kv + cl for cl in range(tiles_per_kv)]
            qs = jnp.concatenate(
                [q_sc[r0:r0 + BLK, c * LANES:(c + 1) * LANES] for c in tiles], axis=0)
            s2 = lax.dot_general(qs, kz, (((1,), (1,)), ((), ())),
                                 preferred_element_type=f32)
            e_rows = []
            sink_rows = []
            for cl, c in enumerate(tiles):
                es = []
                sinks = []
                for p in range(2):
                    h = 2 * c + p
                    s = (s2[cl * BLK:(cl + 1) * BLK, p * 3 * BLK:(p + 1) * 3 * BLK]
                         + bias_sc[variant, h])
                    m = jnp.max(s, axis=-1, keepdims=True)
                    es.append(jnp.exp2(s - m).astype(bf16))
                    sinks.append(jnp.exp2(sink_ref[h] * LOG2_E - m))
                e_rows.append(jnp.concatenate(es, axis=1))
                sink_rows.append(jnp.where(low_half, sinks[0], sinks[1]))
            pv = jnp.dot(jnp.concatenate(e_rows, axis=0),
                         jnp.concatenate([vz, ind_ref[...]], axis=1),
                         preferred_element_type=f32)
            for cl, c in enumerate(tiles):
                rows = slice(cl * BLK, (cl + 1) * BLK)
                att_sc[r0:r0 + BLK, c * LANES:(c + 1) * LANES] = (
                    pv[rows, :LANES] / (pv[rows, LANES:] + sink_rows[cl]))
        return run

    def project_out(half):
        def run():
            rows = slice(half * 2 * BLK, (half + 1) * 2 * BLK)
            a = _row_rms(att_sc[rows, :], a_gain_ref[...]).astype(bf16)
            y = jnp.dot(a, w_o_ref[:ATT_WIDTH, :], preferred_element_type=f32)
            y = y + jnp.dot(g_sc[rows, :], w_o_ref[ATT_WIDTH:, :], preferred_element_type=f32)
            out_ref[0, rows, :] = x_ref[0, rows, :] + y
        return run

    attention = [attend(qb, j) for qb in range(n_blk) for j in range(N_KV)]
    outputs = [project_out(half) for half in range(n_blk // 2)]
    return attention, outputs


def _mixer_kernel(table_ref, sink_ref, bucket_ref, ind_ref, xc_ref, xl_ref, norm1_ref,
                  w_in_ref, q_gain_ref, k_gain_ref, v_gain_ref, w_s_ref, b_s_ref, a_gain_ref,
                  g_gain_ref, w_o_ref, out_ref, bias_sc, att_sc, gu_sc, gv_sc, *stage_sc,
                  blocks_per_seq):
    t = pl.program_id(0)
    set_a, set_b = stage_sc[:4], stage_sc[4:]

    @pl.when(t == 0)
    def _init():
        _build_bias(table_ref, bucket_ref, bias_sc)
        for ref in set_b:
            ref[...] = jnp.zeros(ref.shape, ref.dtype)

    lag = jnp.maximum(t - 1, 0)
    pos = lax.rem(lag, blocks_per_seq)
    seq_first = pos == 0
    seq_last = pos == blocks_per_seq - 1

    def run(write_set, read_set):
        front = _front_pieces(xc_ref, norm1_ref, w_in_ref, q_gain_ref, k_gain_ref, v_gain_ref,
                              w_s_ref, b_s_ref, g_gain_ref, gu_sc, gv_sc, read_set[1],
                              read_set[2], *write_set)
        attention, outputs = _back_pieces(xl_ref, sink_ref, ind_ref, a_gain_ref, w_o_ref, bias_sc,
                                          *read_set, att_sc, out_ref, seq_first, seq_last)
        per_half = len(attention) // len(outputs)
        back = []
        for half, project_out in enumerate(outputs):
            back += attention[half * per_half:(half + 1) * per_half]
            back.append(project_out)
        for i in range(max(len(front), len(back))):
            if i < len(back):
                back[i]()
            if i < len(front):
                front[i]()

    parity = lax.rem(t, 2)

    @pl.when(parity == 0)
    def _even():
        run(set_a, set_b)

    @pl.when(parity == 1)
    def _odd():
        run(set_b, set_a)


def _ffn_kernel(x_ref, norm2_ref, w_gate_ref, w_up_ref, w_down_ref, out_ref):
    f32 = jnp.float32
    x = x_ref[...]
    h = _row_rms(x, norm2_ref[...]).astype(jnp.bfloat16)
    gate = jnp.dot(h, w_gate_ref[...], preferred_element_type=f32)
    up = jnp.dot(h, w_up_ref[...], preferred_element_type=f32)
    act = (gate * (1.0 / (1.0 + jnp.exp(-gate))) * up).astype(jnp.bfloat16)
    out_ref[...] = x + jnp.dot(act, w_down_ref[...], preferred_element_type=f32)


def _resident(shape):
    zeros = (0,) * len(shape)
    return pl.BlockSpec(shape, lambda *_: zeros, pipeline_mode=pl.Buffered(1))


def _mixer(x, p):
    batch, seq, d = x.shape
    tb = MIX_TOKENS
    assert seq % tb == 0 and tb % (2 * BLK) == 0 and seq // BLK >= 2 and d == D_MODEL
    n_i = seq // tb
    n_t = batch * n_i

    def cur(t):
        tc = jnp.minimum(t, n_t - 1)
        return lax.div(tc, n_i), lax.rem(tc, n_i)

    def lagged(t):
        tl = jnp.maximum(t - 1, 0)
        return lax.div(tl, n_i), lax.rem(tl, n_i)

    def cur_map(t):
        b, i = cur(t)
        return b, i, 0

    def lag_map(t):
        b, i = lagged(t)
        return b, i, 0

    smem = pl.BlockSpec(memory_space=pltpu.SMEM)
    in_specs = [
        smem,
        smem,
        _resident((BLK, 3 * BLK)),
        _resident((6 * BLK, LANES)),
        pl.BlockSpec((1, tb, d), cur_map),
        pl.BlockSpec((1, tb, d), lag_map),
        _resident((1, d)),
        _resident((d, IN_WIDTH)),
        _resident((1, ATT_WIDTH)),
        _resident((1, KV_WIDTH)),
        _resident((1, GM_WIDTH)),
        _resident((PAIRS, BLK, 2 * BLK)),
        _resident((PAIRS, BLK, LANES)),
        _resident((1, ATT_WIDTH)),
        _resident((1, GM_WIDTH)),
        _resident((d, d)),
    ]
    stage = [
        pltpu.VMEM((tb, ATT_WIDTH), jnp.bfloat16),
        pltpu.VMEM((2 * N_KV, tb + 2 * BLK, LANES), jnp.bfloat16),
        pltpu.VMEM((2 * N_KV, tb + 2 * BLK, LANES), jnp.bfloat16),
        pltpu.VMEM((tb, GM_WIDTH), jnp.bfloat16),
    ]
    scratch = [
        pltpu.VMEM((3, N_HEADS, BLK, 3 * BLK), jnp.float32),
        pltpu.VMEM((tb, ATT_WIDTH), jnp.float32),
        pltpu.VMEM((tb, GM_WIDTH), jnp.float32),
        pltpu.VMEM((tb, GM_WIDTH), jnp.float32),
    ] + stage + stage
    return pl.pallas_call(
        functools.partial(_mixer_kernel, blocks_per_seq=n_i),
        grid=(n_t + 1,),
        in_specs=in_specs,
        out_specs=pl.BlockSpec((1, tb, d), lag_map),
        out_shape=jax.ShapeDtypeStruct(x.shape, x.dtype),
        scratch_shapes=scratch,
        compiler_params=pltpu.CompilerParams(
            dimension_semantics=("arbitrary",),
            vmem_limit_bytes=VMEM_LIMIT_BYTES),
        name="mixer",
    )(p["table"], p["sink"], p["bucket"], p["indicator"], x, x, p["norm1"], p["w_in"], p["q_gain"],
      p["k_gain"], p["v_gain"], p["w_s"], p["b_s"], p["a_gain"], p["g_gain"], p["w_o"])


def _ffn(x2d, p):
    n, d = x2d.shape
    tm = FFN_TOKENS
    assert n % tm == 0
    d_ff = p["w_gate"].shape[1]
    return pl.pallas_call(
        _ffn_kernel,
        grid=(n // tm,),
        in_specs=[
            pl.BlockSpec((tm, d), lambda i: (i, 0)),
            _resident((1, d)),
            _resident((d, d_ff)),
            _resident((d, d_ff)),
            _resident((d_ff, d)),
        ],
        out_specs=pl.BlockSpec((tm, d), lambda i: (i, 0)),
        out_shape=jax.ShapeDtypeStruct(x2d.shape, x2d.dtype),
        compiler_params=pltpu.CompilerParams(
            dimension_semantics=("arbitrary",),
            vmem_limit_bytes=VMEM_LIMIT_BYTES),
        name="ffn",
    )(x2d, p["norm2"], p["w_gate"], p["w_up"], p["w_down"])


def _layer_params(rel_bias_table, norm1, w_in, q_gain, k_gain, sink, v_gain, w_s, b_s,
                  attn_out_gain, gmlp_out_gain, w_o, norm2, w_gate, w_up, w_down):
    bf16 = jnp.bfloat16
    row = lambda v: v.reshape(1, -1).astype(jnp.float32)
    w_pairs = jnp.concatenate([w_s[0::2], w_s[1::2]], axis=-1).astype(bf16)
    b_pairs = jnp.repeat(jnp.transpose(b_s.reshape(PAIRS, 2, BLK), (0, 2, 1)), HEAD_DIM, axis=-1)
    return {
        "table": rel_bias_table.astype(jnp.float32),
        "sink": sink.astype(jnp.float32),
        "bucket": jnp.asarray(_t5_bucket_table()),
        "indicator": jnp.asarray(_key_set_indicator(), dtype=bf16),
        "norm1": row(norm1),
        "w_in": w_in.astype(bf16),
        "q_gain": row(jnp.tile(q_gain, N_HEADS)),
        "k_gain": row(jnp.tile(k_gain, N_KV)),
        "v_gain": row(v_gain),
        "w_s": w_pairs,
        "b_s": b_pairs.astype(jnp.float32),
        "a_gain": row(attn_out_gain),
        "g_gain": row(gmlp_out_gain),
        "w_o": w_o.astype(bf16),
        "norm2": row(norm2),
        "w_gate": w_gate.astype(bf16),
        "w_up": w_up.astype(bf16),
        "w_down": w_down.astype(bf16),
    }


def kernel(x_prompt, x_sample, rel_bias_table, norm1, w_in, q_gain, k_gain, sink, v_gain, w_s, b_s,
           attn_out_gain, gmlp_out_gain, w_o, norm2, w_gate, w_up, w_down):
    layers = [
        _layer_params(rel_bias_table, norm1[l], w_in[l], q_gain[l], k_gain[l], sink[l], v_gain[l],
                      w_s[l], b_s[l], attn_out_gain[l], gmlp_out_gain[l], w_o[l], norm2[l],
                      w_gate[l], w_up[l], w_down[l])
        for l in range(norm1.shape[0])]
    outs = []
    for x in (x_prompt, x_sample):
        for p in layers:
            x = _mixer(x, p)
            x = _ffn(x.reshape(-1, D_MODEL), p).reshape(x.shape)
        outs.append(x)
    return tuple(outs)
```

```python
import functools
import math

import jax
import jax.numpy as jnp
import numpy as np
from jax import lax
from jax.experimental import pallas as pl
from jax.experimental.pallas import tpu as pltpu

D_MODEL = 1024
HEAD_DIM = 64
N_HEADS = 8
N_KV = 2
ATT_WIDTH = N_HEADS * HEAD_DIM
KV_WIDTH = N_KV * HEAD_DIM
GM_HEADS = 8
GM_WIDTH = GM_HEADS * HEAD_DIM
IN_WIDTH = ATT_WIDTH + 2 * KV_WIDTH + 2 * GM_WIDTH
BLK = 128
N_BUCKETS = 32
MAX_DIST = 128
EPS = 1e-6
NEG = -1e30
LOG2_E = math.log2(math.e)

LANES = 128
SUBLANES = 8
PAIRS = ATT_WIDTH // LANES
MIX_TOKENS = 1024
FFN_TOKENS = 512
VMEM_LIMIT_BYTES = 56 * 1024 * 1024

_Q_OFF = 0
_KV_OFF = ATT_WIDTH
_G_OFF = ATT_WIDTH + 2 * KV_WIDTH


def _t5_bucket_table():
    rel = (np.arange(3 * BLK) - BLK)[None, :] - np.arange(BLK)[:, None]
    half = N_BUCKETS // 2
    max_exact = half // 2
    ret = (rel > 0).astype(np.int32) * half
    n = np.abs(rel)
    large = max_exact + (np.log(np.maximum(n, 1).astype(np.float32) / max_exact)
                         / np.log(MAX_DIST / max_exact) * (half - max_exact)).astype(np.int32)
    large = np.minimum(large, half - 1)
    bucket = (ret + np.where(n < max_exact, n, large)).astype(np.int32)
    return np.where(np.abs(rel) <= BLK, bucket, -1).astype(np.int32)


def _key_set_indicator():
    key_is_low = np.arange(6 * BLK)[:, None] < 3 * BLK
    lane_is_low = np.arange(LANES)[None, :] < HEAD_DIM
    return (key_is_low == lane_is_low).astype(np.float32)


def _row_rms(x, gain):
    ms = jnp.mean(x * x, axis=-1, keepdims=True)
    return x * lax.rsqrt(ms + EPS) * gain


def _head_rms(x, gain, low_half):
    x2 = x * x
    s_lo = jnp.sum(jnp.where(low_half, x2, 0.0), axis=-1, keepdims=True)
    s_hi = jnp.sum(jnp.where(low_half, 0.0, x2), axis=-1, keepdims=True)
    r_lo = lax.rsqrt(s_lo * (1.0 / HEAD_DIM) + EPS)
    r_hi = lax.rsqrt(s_hi * (1.0 / HEAD_DIM) + EPS)
    return x * jnp.where(low_half, r_lo, r_hi) * gain


def _gelu_tanh(x):
    a = -2.0 * math.sqrt(2.0 / math.pi) * LOG2_E
    w = x * (x * x * (a * 0.044715) + a)
    return x * (1.0 / (1.0 + jnp.exp2(w)))


def _build_bias(table_ref, bucket_ref, bias_sc):
    f32 = jnp.float32
    neg = jnp.full((BLK, BLK), NEG, f32)
    for h in range(N_HEADS):
        for part in range(3):
            bk = bucket_ref[:, part * BLK:(part + 1) * BLK]

            def body(b, acc):
                return jnp.where(bk == b, table_ref[b, h] * LOG2_E, acc)

            acc = lax.fori_loop(0, N_BUCKETS, body, neg)
            sl = slice(part * BLK, (part + 1) * BLK)
            bias_sc[0, h, :, sl] = acc
            bias_sc[1, h, :, sl] = neg if part == 0 else acc
            bias_sc[2, h, :, sl] = neg if part == 2 else acc


def _front_pieces(xc_ref, norm1_ref, w_in_ref, q_gain_ref, k_gain_ref, v_gain_ref, w_s_ref,
                  b_s_ref, g_gain_ref, gu_sc, gv_sc, kz_prev, vz_prev, q_sc, kz_sc, vz_sc, g_sc):
    f32 = jnp.float32
    bf16 = jnp.bfloat16
    tb = xc_ref.shape[1]
    n_blk = tb // BLK
    low_half = lax.broadcasted_iota(jnp.int32, (1, LANES), 1) < HEAD_DIM
    wide = 2 * LANES
    state = {}

    def project(h, off):
        return jnp.dot(h, w_in_ref[:, off:off + wide], preferred_element_type=f32)

    def keys_values():
        hc = _row_rms(xc_ref[0], norm1_ref[:1, :]).astype(bf16)
        state["hc"] = hc
        kv = project(hc, _KV_OFF)
        kn = _head_rms(kv[:, :KV_WIDTH], k_gain_ref[:1, :KV_WIDTH], low_half)
        vv = kv[:, KV_WIDTH:]
        kn_sw = pltpu.roll(kn, HEAD_DIM, axis=1)
        vv_sw = pltpu.roll(vv, HEAD_DIM, axis=1)
        zero = jnp.zeros_like(kn)
        tail = slice(tb, tb + BLK)
        for idx, (src_k, src_v, keep_low) in enumerate(
                ((kn, vv, True), (kn_sw, vv_sw, False), (kn_sw, vv_sw, True), (kn, vv, False))):
            keep = low_half if keep_low else jnp.logical_not(low_half)
            for src, dst, dst_prev in ((src_k, kz_sc, kz_prev), (src_v, vz_sc, vz_prev)):
                laid = jnp.where(keep, src, zero).astype(bf16)
                dst[idx, BLK:BLK + tb, :] = laid
                dst_prev[idx, BLK + tb:, :] = laid[:BLK]
                dst[idx, :BLK, :] = dst_prev[idx, tail, :]

    def queries(tile):
        def run():
            qf = project(state["hc"], _Q_OFF + tile * wide)
            for cl in range(wide // LANES):
                sl = slice(tile * wide + cl * LANES, tile * wide + (cl + 1) * LANES)
                gain = q_gain_ref[:1, sl] * (LOG2_E / math.sqrt(HEAD_DIM))
                q_sc[:, sl] = _head_rms(qf[:, cl * LANES:(cl + 1) * LANES], gain,
                                        low_half).astype(bf16)
        return run

    def gelu_tile(tile):
        def run():
            gf = _gelu_tanh(project(state["hc"], _G_OFF + tile * wide))
            dst, col = (gu_sc, tile * wide) if tile * wide < GM_WIDTH else (
                gv_sc, tile * wide - GM_WIDTH)
            dst[:, col:col + wide] = gf
        return run

    def gate_rows(r):
        def run():
            rows = slice(r * BLK, (r + 1) * BLK)
            gv = _row_rms(gv_sc[rows, :], v_gain_ref[:1, :])
            tiles = []
            for c in range(PAIRS):
                sl = slice(c * LANES, (c + 1) * LANES)
                vt = gv[:, sl]
                stacked = jnp.concatenate(
                    [jnp.where(low_half, vt, 0.0), jnp.where(low_half, 0.0, vt)],
                    axis=0).astype(bf16)
                s = jnp.dot(w_s_ref[c], stacked, preferred_element_type=f32) + b_s_ref[c]
                tiles.append(gu_sc[rows, sl] * s)
            gm = jnp.concatenate(tiles, axis=1)
            g_sc[rows, :] = _row_rms(gm, g_gain_ref[:1, :]).astype(bf16)
        return run

    pieces = [keys_values]
    pieces += [queries(tile) for tile in range(ATT_WIDTH // wide)]
    pieces += [gelu_tile(tile) for tile in range(2 * GM_WIDTH // wide)]
    pieces += [gate_rows(r) for r in range(n_blk)]
    return pieces


def _back_pieces(sink_ref, ind_ref, a_gain_ref, w_o_ref, bias_sc, q_sc, kz_sc, vz_sc, g_sc,
                 att_sc, out_ref, seq_first, seq_last):
    f32 = jnp.float32
    bf16 = jnp.bfloat16
    n_blk = att_sc.shape[0] // BLK
    low_half = lax.broadcasted_iota(jnp.int32, (1, LANES), 1) < HEAD_DIM
    tiles_per_kv = PAIRS // N_KV

    def attend(qb, j):
        def run():
            r0 = qb * BLK
            if qb == 0:
                variant = jnp.where(seq_first, 1, 0)
            elif qb == n_blk - 1:
                variant = jnp.where(seq_last, 2, 0)
            else:
                variant = 0
            band = slice(r0, r0 + 3 * BLK)
            kz = jnp.concatenate([kz_sc[2 * j, band, :], kz_sc[2 * j + 1, band, :]], axis=0)
            vz = jnp.concatenate([vz_sc[2 * j, band, :], vz_sc[2 * j + 1, band, :]], axis=0)
            tiles = [j * tiles_per_kv + cl for cl in range(tiles_per_kv)]
            qs = jnp.concatenate(
                [q_sc[r0:r0 + BLK, c * LANES:(c + 1) * LANES] for c in tiles], axis=0)
            s2 = lax.dot_general(qs, kz, (((1,), (1,)), ((), ())),
                                 preferred_element_type=f32)
            e_rows = []
            sink_rows = []
            for cl, c in enumerate(tiles):
                es = []
                sinks = []
                for p in range(2):
                    h = 2 * c + p
                    s = (s2[cl * BLK:(cl + 1) * BLK, p * 3 * BLK:(p + 1) * 3 * BLK]
                         + bias_sc[variant, h])
                    m = jnp.max(s, axis=-1, keepdims=True)
                    es.append(jnp.exp2(s - m).astype(bf16))
                    sinks.append(jnp.exp2(sink_ref[h] * LOG2_E - m))
                e_rows.append(jnp.concatenate(es, axis=1))
                sink_rows.append(jnp.where(low_half, sinks[0], sinks[1]))
            pv = jnp.dot(jnp.concatenate(e_rows, axis=0),
                         jnp.concatenate([vz, ind_ref[...]], axis=1),
                         preferred_element_type=f32)
            for cl, c in enumerate(tiles):
                rows = slice(cl * BLK, (cl + 1) * BLK)
                att_sc[r0:r0 + BLK, c * LANES:(c + 1) * LANES] = (
                    pv[rows, :LANES] / (pv[rows, LANES:] + sink_rows[cl]))
        return run

    def project_out(half):
        def run():
            rows = slice(half * 2 * BLK, (half + 1) * 2 * BLK)
            a = _row_rms(att_sc[rows, :], a_gain_ref[:1, :]).astype(bf16)
            y = jnp.dot(a, w_o_ref[:ATT_WIDTH, :], preferred_element_type=f32)
            y = y + jnp.dot(g_sc[rows, :], w_o_ref[ATT_WIDTH:, :], preferred_element_type=f32)
            out_ref[0, rows, :] = y
        return run

    attention = [attend(qb, j) for qb in range(n_blk) for j in range(N_KV)]
    outputs = [project_out(half) for half in range(n_blk // 2)]
    return attention, outputs


def _mixer_kernel(table_ref, sink_ref, bucket_ref, ind_ref, xc_ref, norm1_ref,
                  w_in_ref, q_gain_ref, k_gain_ref, v_gain_ref, w_s_ref, b_s_ref, a_gain_ref,
                  g_gain_ref, w_o_ref, out_ref, bias_sc, att_sc, gu_sc, gv_sc, *stage_sc,
                  blocks_per_seq):
    t = pl.program_id(0)
    set_a, set_b = stage_sc[:4], stage_sc[4:]

    @pl.when(t == 0)
    def _init():
        _build_bias(table_ref, bucket_ref, bias_sc)
        for ref in set_b:
            ref[...] = jnp.zeros(ref.shape, ref.dtype)

    lag = jnp.maximum(t - 1, 0)
    pos = lax.rem(lag, blocks_per_seq)
    seq_first = pos == 0
    seq_last = pos == blocks_per_seq - 1

    def run(write_set, read_set):
        front = _front_pieces(xc_ref, norm1_ref, w_in_ref, q_gain_ref, k_gain_ref, v_gain_ref,
                              w_s_ref, b_s_ref, g_gain_ref, gu_sc, gv_sc, read_set[1],
                              read_set[2], *write_set)
        attention, outputs = _back_pieces(sink_ref, ind_ref, a_gain_ref, w_o_ref, bias_sc,
                                          *read_set, att_sc, out_ref, seq_first, seq_last)
        per_half = len(attention) // len(outputs)
        back = []
        for half, project_out in enumerate(outputs):
            back += attention[half * per_half:(half + 1) * per_half]
            back.append(project_out)
        for i in range(max(len(front), len(back))):
            if i < len(back):
                back[i]()
            if i < len(front):
                front[i]()

    parity = lax.rem(t, 2)

    @pl.when(parity == 0)
    def _even():
        run(set_a, set_b)

    @pl.when(parity == 1)
    def _odd():
        run(set_b, set_a)


def _ffn_kernel(x_ref, mix_ref, norm2_ref, w_gate_ref, w_up_ref, w_down_ref, out_ref):
    f32 = jnp.float32
    x = x_ref[...] + mix_ref[...]
    h = _row_rms(x, norm2_ref[:1, :]).astype(jnp.bfloat16)
    gate = jnp.dot(h, w_gate_ref[...], preferred_element_type=f32)
    up = jnp.dot(h, w_up_ref[...], preferred_element_type=f32)
    act = (gate * (1.0 / (1.0 + jnp.exp(-gate))) * up).astype(jnp.bfloat16)
    out_ref[...] = x + jnp.dot(act, w_down_ref[...], preferred_element_type=f32)


def _resident(shape):
    zeros = (0,) * len(shape)
    return pl.BlockSpec(shape, lambda *_: zeros, pipeline_mode=pl.Buffered(1))


def _mixer(x, p):
    batch, seq, d = x.shape
    tb = MIX_TOKENS
    assert seq % tb == 0 and tb % (2 * BLK) == 0 and seq // BLK >= 2 and d == D_MODEL
    n_i = seq // tb
    n_t = batch * n_i

    def cur(t):
        tc = jnp.minimum(t, n_t - 1)
        return lax.div(tc, n_i), lax.rem(tc, n_i)

    def lagged(t):
        tl = jnp.maximum(t - 1, 0)
        return lax.div(tl, n_i), lax.rem(tl, n_i)

    def cur_map(t):
        b, i = cur(t)
        return b, i, 0

    def lag_map(t):
        b, i = lagged(t)
        return b, i, 0

    smem = pl.BlockSpec(memory_space=pltpu.SMEM)
    in_specs = [
        smem,
        smem,
        _resident((BLK, 3 * BLK)),
        _resident((6 * BLK, LANES)),
        pl.BlockSpec((1, tb, d), cur_map),
        _resident((SUBLANES, d)),
        _resident((d, IN_WIDTH)),
        _resident((SUBLANES, ATT_WIDTH)),
        _resident((SUBLANES, ATT_WIDTH)),
        _resident((SUBLANES, GM_WIDTH)),
        _resident((PAIRS, BLK, 2 * BLK)),
        _resident((PAIRS, BLK, LANES)),
        _resident((SUBLANES, ATT_WIDTH)),
        _resident((SUBLANES, GM_WIDTH)),
        _resident((d, d)),
    ]
    stage = [
        pltpu.VMEM((tb, ATT_WIDTH), jnp.bfloat16),
        pltpu.VMEM((2 * N_KV, tb + 2 * BLK, LANES), jnp.bfloat16),
        pltpu.VMEM((2 * N_KV, tb + 2 * BLK, LANES), jnp.bfloat16),
        pltpu.VMEM((tb, GM_WIDTH), jnp.bfloat16),
    ]
    scratch = [
        pltpu.VMEM((3, N_HEADS, BLK, 3 * BLK), jnp.float32),
        pltpu.VMEM((tb, ATT_WIDTH), jnp.float32),
        pltpu.VMEM((tb, GM_WIDTH), jnp.float32),
        pltpu.VMEM((tb, GM_WIDTH), jnp.float32),
    ] + stage + stage
    return pl.pallas_call(
        functools.partial(_mixer_kernel, blocks_per_seq=n_i),
        grid=(n_t + 1,),
        in_specs=in_specs,
        out_specs=pl.BlockSpec((1, tb, d), lag_map),
        out_shape=jax.ShapeDtypeStruct(x.shape, x.dtype),
        scratch_shapes=scratch,
        compiler_params=pltpu.CompilerParams(
            dimension_semantics=("arbitrary",),
            vmem_limit_bytes=VMEM_LIMIT_BYTES),
        name="mixer",
    )(p["table"], p["sink"], p["bucket"], p["indicator"], x, p["norm1"], p["w_in"], p["q_gain"],
      p["k_gain"], p["v_gain"], p["w_s"], p["b_s"], p["a_gain"], p["g_gain"], p["w_o"])


def _ffn(x2d, mix2d, p):
    n, d = x2d.shape
    tm = FFN_TOKENS
    assert n % tm == 0
    d_ff = p["w_gate"].shape[1]
    return pl.pallas_call(
        _ffn_kernel,
        grid=(n // tm,),
        in_specs=[
            pl.BlockSpec((tm, d), lambda i: (i, 0)),
            pl.BlockSpec((tm, d), lambda i: (i, 0)),
            _resident((SUBLANES, d)),
            _resident((d, d_ff)),
            _resident((d, d_ff)),
            _resident((d_ff, d)),
        ],
        out_specs=pl.BlockSpec((tm, d), lambda i: (i, 0)),
        out_shape=jax.ShapeDtypeStruct(x2d.shape, x2d.dtype),
        compiler_params=pltpu.CompilerParams(
            dimension_semantics=("arbitrary",),
            vmem_limit_bytes=VMEM_LIMIT_BYTES),
        name="ffn",
    )(x2d, mix2d, p["norm2"], p["w_gate"], p["w_up"], p["w_down"])


def _layer_params(rel_bias_table, norm1, w_in, q_gain, k_gain, sink, v_gain, w_s, b_s,
                  attn_out_gain, gmlp_out_gain, w_o, norm2, w_gate, w_up, w_down):
    bf16 = jnp.bfloat16

    def row(v, width=None):
        v = v.reshape(1, -1).astype(jnp.float32)
        if width is not None:
            v = jnp.pad(v, ((0, 0), (0, width - v.shape[1])))
        return jnp.broadcast_to(v, (SUBLANES, v.shape[1]))

    w_pairs = jnp.concatenate([w_s[0::2], w_s[1::2]], axis=-1).astype(bf16)
    b_pairs = jnp.repeat(jnp.transpose(b_s.reshape(PAIRS, 2, BLK), (0, 2, 1)), HEAD_DIM, axis=-1)
    return {
        "table": rel_bias_table.astype(jnp.float32),
        "sink": sink.astype(jnp.float32),
        "bucket": jnp.asarray(_t5_bucket_table()),
        "indicator": jnp.asarray(_key_set_indicator(), dtype=bf16),
        "norm1": row(norm1),
        "w_in": w_in.astype(bf16),
        "q_gain": row(jnp.tile(q_gain, N_HEADS)),
        "k_gain": row(jnp.tile(k_gain, N_KV), ATT_WIDTH),
        "v_gain": row(v_gain),
        "w_s": w_pairs,
        "b_s": b_pairs.astype(jnp.float32),
        "a_gain": row(attn_out_gain),
        "g_gain": row(gmlp_out_gain),
        "w_o": w_o.astype(bf16),
        "norm2": row(norm2),
        "w_gate": w_gate.astype(bf16),
        "w_up": w_up.astype(bf16),
        "w_down": w_down.astype(bf16),
    }


def kernel(x_prompt, x_sample, rel_bias_table, norm1, w_in, q_gain, k_gain, sink, v_gain, w_s, b_s,
           attn_out_gain, gmlp_out_gain, w_o, norm2, w_gate, w_up, w_down):
    layers = [
        _layer_params(rel_bias_table, norm1[l], w_in[l], q_gain[l], k_gain[l], sink[l], v_gain[l],
                      w_s[l], b_s[l], attn_out_gain[l], gmlp_out_gain[l], w_o[l], norm2[l],
                      w_gate[l], w_up[l], w_down[l])
        for l in range(norm1.shape[0])]
    outs = []
    for x in (x_prompt, x_sample):
        for p in layers:
            mix = _mixer(x, p)
            x = _ffn(x.reshape(-1, D_MODEL), mix.reshape(-1, D_MODEL), p).reshape(x.shape)
        outs.append(x)
    return tuple(outs)
```

```python
import functools
import math

import jax
import jax.numpy as jnp
import numpy as np
from jax import lax
from jax.experimental import pallas as pl
from jax.experimental.pallas import tpu as pltpu

D_MODEL = 1024
HEAD_DIM = 64
N_HEADS = 8
N_KV = 2
ATT_WIDTH = N_HEADS * HEAD_DIM
KV_WIDTH = N_KV * HEAD_DIM
GM_HEADS = 8
GM_WIDTH = GM_HEADS * HEAD_DIM
IN_WIDTH = ATT_WIDTH + 2 * KV_WIDTH + 2 * GM_WIDTH
BLK = 128
N_BUCKETS = 32
MAX_DIST = 128
EPS = 1e-6
NEG = -1e30
LOG2_E = math.log2(math.e)

LANES = 128
SUBLANES = 8
PAIRS = ATT_WIDTH // LANES
MIX_TOKENS = 512
FFN_TOKENS = 512
VMEM_LIMIT_BYTES = 56 * 1024 * 1024

_Q_OFF = 0
_KV_OFF = ATT_WIDTH
_G_OFF = ATT_WIDTH + 2 * KV_WIDTH


def _t5_bucket_table():
    rel = (np.arange(3 * BLK) - BLK)[None, :] - np.arange(BLK)[:, None]
    half = N_BUCKETS // 2
    max_exact = half // 2
    ret = (rel > 0).astype(np.int32) * half
    n = np.abs(rel)
    large = max_exact + (np.log(np.maximum(n, 1).astype(np.float32) / max_exact)
                         / np.log(MAX_DIST / max_exact) * (half - max_exact)).astype(np.int32)
    large = np.minimum(large, half - 1)
    bucket = (ret + np.where(n < max_exact, n, large)).astype(np.int32)
    return np.where(np.abs(rel) <= BLK, bucket, -1).astype(np.int32)


def _key_set_indicator():
    key_is_low = np.arange(6 * BLK)[:, None] < 3 * BLK
    lane_is_low = np.arange(LANES)[None, :] < HEAD_DIM
    return (key_is_low == lane_is_low).astype(np.float32)


def _row_rms(x, gain):
    ms = jnp.mean(x * x, axis=-1, keepdims=True)
    return x * lax.rsqrt(ms + EPS) * gain


def _head_rms(x, gain, low_half):
    x2 = x * x
    s_lo = jnp.sum(jnp.where(low_half, x2, 0.0), axis=-1, keepdims=True)
    s_hi = jnp.sum(jnp.where(low_half, 0.0, x2), axis=-1, keepdims=True)
    r_lo = lax.rsqrt(s_lo * (1.0 / HEAD_DIM) + EPS)
    r_hi = lax.rsqrt(s_hi * (1.0 / HEAD_DIM) + EPS)
    return x * jnp.where(low_half, r_lo, r_hi) * gain


def _gelu_tanh(x):
    a = -2.0 * math.sqrt(2.0 / math.pi) * LOG2_E
    w = x * (x * x * (a * 0.044715) + a)
    return x * (1.0 / (1.0 + jnp.exp2(w)))


def _build_bias(table_ref, bucket_ref, bias_sc):
    f32 = jnp.float32
    neg = jnp.full((BLK, BLK), NEG, f32)
    for h in range(N_HEADS):
        for part in range(3):
            bk = bucket_ref[:, part * BLK:(part + 1) * BLK]

            def body(b, acc):
                return jnp.where(bk == b, table_ref[b, h] * LOG2_E, acc)

            acc = lax.fori_loop(0, N_BUCKETS, body, neg)
            sl = slice(part * BLK, (part + 1) * BLK)
            bias_sc[0, h, :, sl] = acc
            bias_sc[1, h, :, sl] = neg if part == 0 else acc
            bias_sc[2, h, :, sl] = neg if part == 2 else acc


def _front_pieces(xc_ref, norm1_ref, w_in_ref, q_gain_ref, k_gain_ref, v_gain_ref, w_s_ref,
                  b_s_ref, g_gain_ref, gu_sc, gv_sc, kz_prev, vz_prev, q_sc, kz_sc, vz_sc, g_sc):
    f32 = jnp.float32
    bf16 = jnp.bfloat16
    tb = xc_ref.shape[1]
    n_blk = tb // BLK
    low_half = lax.broadcasted_iota(jnp.int32, (1, LANES), 1) < HEAD_DIM
    wide = 2 * LANES
    state = {}

    def project(h, off):
        return jnp.dot(h, w_in_ref[:, off:off + wide], preferred_element_type=f32)

    def keys_values():
        hc = _row_rms(xc_ref[0], norm1_ref[:1, :]).astype(bf16)
        state["hc"] = hc
        kv = project(hc, _KV_OFF)
        kn = _head_rms(kv[:, :KV_WIDTH], k_gain_ref[:1, :KV_WIDTH], low_half)
        vv = kv[:, KV_WIDTH:]
        kn_sw = pltpu.roll(kn, HEAD_DIM, axis=1)
        vv_sw = pltpu.roll(vv, HEAD_DIM, axis=1)
        zero = jnp.zeros_like(kn)
        tail = slice(tb, tb + BLK)
        for idx, (src_k, src_v, keep_low) in enumerate(
                ((kn, vv, True), (kn_sw, vv_sw, False), (kn_sw, vv_sw, True), (kn, vv, False))):
            keep = low_half if keep_low else jnp.logical_not(low_half)
            for src, dst, dst_prev in ((src_k, kz_sc, kz_prev), (src_v, vz_sc, vz_prev)):
                laid = jnp.where(keep, src, zero).astype(bf16)
                dst[idx, BLK:BLK + tb, :] = laid
                dst_prev[idx, BLK + tb:, :] = laid[:BLK]
                dst[idx, :BLK, :] = dst_prev[idx, tail, :]

    def queries(tile):
        def run():
            qf = project(state["hc"], _Q_OFF + tile * wide)
            for cl in range(wide // LANES):
                sl = slice(tile * wide + cl * LANES, tile * wide + (cl + 1) * LANES)
                gain = q_gain_ref[:1, sl] * (LOG2_E / math.sqrt(HEAD_DIM))
                q_sc[:, sl] = _head_rms(qf[:, cl * LANES:(cl + 1) * LANES], gain,
                                        low_half).astype(bf16)
        return run

    def gelu_tile(tile):
        def run():
            gf = _gelu_tanh(project(state["hc"], _G_OFF + tile * wide))
            dst, col = (gu_sc, tile * wide) if tile * wide < GM_WIDTH else (
                gv_sc, tile * wide - GM_WIDTH)
            dst[:, col:col + wide] = gf
        return run

    def gate_rows(r):
        def run():
            rows = slice(r * BLK, (r + 1) * BLK)
            gv = _row_rms(gv_sc[rows, :], v_gain_ref[:1, :])
            tiles = []
            for c in range(PAIRS):
                sl = slice(c * LANES, (c + 1) * LANES)
                vt = gv[:, sl]
                stacked = jnp.concatenate(
                    [jnp.where(low_half, vt, 0.0), jnp.where(low_half, 0.0, vt)],
                    axis=0).astype(bf16)
                s = jnp.dot(w_s_ref[c], stacked, preferred_element_type=f32) + b_s_ref[c]
                tiles.append(gu_sc[rows, sl] * s)
            gm = jnp.concatenate(tiles, axis=1)
            g_sc[rows, :] = _row_rms(gm, g_gain_ref[:1, :]).astype(bf16)
        return run

    early = [keys_values] + [gelu_tile(tile) for tile in range(2 * GM_WIDTH // wide)]
    gates = [gate_rows(r) for r in range(n_blk)]
    queries_ = [queries(tile) for tile in range(ATT_WIDTH // wide)]
    return early, gates, queries_


def _back_pieces(x_ref, sink_ref, ind_ref, a_gain_ref, w_o_ref, bias_sc, q_sc, kz_sc, vz_sc, g_sc,
                 att_sc, out_ref, seq_first, seq_last):
    f32 = jnp.float32
    bf16 = jnp.bfloat16
    n_blk = x_ref.shape[1] // BLK
    low_half = lax.broadcasted_iota(jnp.int32, (1, LANES), 1) < HEAD_DIM
    tiles_per_kv = PAIRS // N_KV

    def attend(qb, j):
        r0 = qb * BLK
        tiles = [j * tiles_per_kv + cl for cl in range(tiles_per_kv)]
        band = slice(r0, r0 + 3 * BLK)
        held = {}

        def scores():
            if qb == 0:
                variant = jnp.where(seq_first, 1, 0)
            elif qb == n_blk - 1:
                variant = jnp.where(seq_last, 2, 0)
            else:
                variant = 0
            kz = jnp.concatenate([kz_sc[2 * j, band, :], kz_sc[2 * j + 1, band, :]], axis=0)
            qs = jnp.concatenate(
                [q_sc[r0:r0 + BLK, c * LANES:(c + 1) * LANES] for c in tiles], axis=0)
            s2 = lax.dot_general(qs, kz, (((1,), (1,)), ((), ())),
                                 preferred_element_type=f32)
            e_rows = []
            sink_rows = []
            for cl, c in enumerate(tiles):
                es = []
                sinks = []
                for p in range(2):
                    h = 2 * c + p
                    s = (s2[cl * BLK:(cl + 1) * BLK, p * 3 * BLK:(p + 1) * 3 * BLK]
                         + bias_sc[variant, h])
                    m = jnp.max(s, axis=-1, keepdims=True)
                    es.append(jnp.exp2(s - m).astype(bf16))
                    sinks.append(jnp.exp2(sink_ref[h] * LOG2_E - m))
                e_rows.append(jnp.concatenate(es, axis=1))
                sink_rows.append(jnp.where(low_half, sinks[0], sinks[1]))
            held["e"] = jnp.concatenate(e_rows, axis=0)
            held["sink"] = sink_rows

        def values():
            vz = jnp.concatenate([vz_sc[2 * j, band, :], vz_sc[2 * j + 1, band, :]], axis=0)
            pv = jnp.dot(held["e"], jnp.concatenate([vz, ind_ref[...]], axis=1),
                         preferred_element_type=f32)
            for cl, c in enumerate(tiles):
                rows = slice(cl * BLK, (cl + 1) * BLK)
                att_sc[r0:r0 + BLK, c * LANES:(c + 1) * LANES] = (
                    pv[rows, :LANES] / (pv[rows, LANES:] + held["sink"][cl]))

        return scores, values

    def project_out(half):
        def run():
            rows = slice(half * 2 * BLK, (half + 1) * 2 * BLK)
            a = _row_rms(att_sc[rows, :], a_gain_ref[:1, :]).astype(bf16)
            y = jnp.dot(a, w_o_ref[:ATT_WIDTH, :], preferred_element_type=f32)
            y = y + jnp.dot(g_sc[rows, :], w_o_ref[ATT_WIDTH:, :], preferred_element_type=f32)
            out_ref[0, rows, :] = x_ref[0, rows, :] + y
        return run

    attention = [attend(qb, j) for qb in range(n_blk) for j in range(N_KV)]
    outputs = [project_out(half) for half in range(n_blk // 2)]
    return attention, outputs


def _mixer_kernel(table_ref, sink_ref, bucket_ref, ind_ref, xc_ref, xl_ref, norm1_ref,
                  w_in_ref, q_gain_ref, k_gain_ref, v_gain_ref, w_s_ref, b_s_ref, a_gain_ref,
                  g_gain_ref, w_o_ref, out_ref, bias_sc, att_sc, gu_sc, gv_sc, *stage_sc,
                  blocks_per_seq):
    t = pl.program_id(0)
    set_a, set_b = stage_sc[:4], stage_sc[4:]

    @pl.when(t == 0)
    def _init():
        _build_bias(table_ref, bucket_ref, bias_sc)
        for ref in set_b:
            ref[...] = jnp.zeros(ref.shape, ref.dtype)

    lag = jnp.maximum(t - 1, 0)
    pos = lax.rem(lag, blocks_per_seq)
    seq_first = pos == 0
    seq_last = pos == blocks_per_seq - 1

    def run(write_set, read_set):
        early, gates, query_tiles = _front_pieces(
            xc_ref, norm1_ref, w_in_ref, q_gain_ref, k_gain_ref, v_gain_ref, w_s_ref, b_s_ref,
            g_gain_ref, gu_sc, gv_sc, read_set[1], read_set[2], *write_set)
        attention, outputs = _back_pieces(xl_ref, sink_ref, ind_ref, a_gain_ref, w_o_ref, bias_sc,
                                          *read_set, att_sc, out_ref, seq_first, seq_last)
        early = iter(early)
        for i, (scores, _) in enumerate(attention + [(None, None)]):
            if scores is not None:
                scores()
                next(early, lambda: None)()
            if i >= 1:
                attention[i - 1][1]()
        tail = list(early)
        assert len(gates) == 2 * len(outputs) == 2 * len(query_tiles)
        for group in zip(outputs, gates[0::2], query_tiles, gates[1::2]):
            tail += group
        for piece in tail:
            piece()

    parity = lax.rem(t, 2)

    @pl.when(parity == 0)
    def _even():
        run(set_a, set_b)

    @pl.when(parity == 1)
    def _odd():
        run(set_b, set_a)


def _ffn_kernel(x_ref, norm2_ref, w_gate_ref, w_up_ref, w_down_ref, out_ref):
    f32 = jnp.float32
    x = x_ref[...]
    h = _row_rms(x, norm2_ref[:1, :]).astype(jnp.bfloat16)
    gate = jnp.dot(h, w_gate_ref[...], preferred_element_type=f32)
    up = jnp.dot(h, w_up_ref[...], preferred_element_type=f32)
    act = (gate * (1.0 / (1.0 + jnp.exp(-gate))) * up).astype(jnp.bfloat16)
    out_ref[...] = x + jnp.dot(act, w_down_ref[...], preferred_element_type=f32)


def _resident(shape):
    zeros = (0,) * len(shape)
    return pl.BlockSpec(shape, lambda *_: zeros, pipeline_mode=pl.Buffered(1))


def _mixer(x, p):
    batch, seq, d = x.shape
    tb = MIX_TOKENS
    assert seq % tb == 0 and tb % (2 * BLK) == 0 and seq // BLK >= 2 and d == D_MODEL
    n_i = seq // tb
    n_t = batch * n_i

    def cur(t):
        tc = jnp.minimum(t, n_t - 1)
        return lax.div(tc, n_i), lax.rem(tc, n_i)

    def lagged(t):
        tl = jnp.maximum(t - 1, 0)
        return lax.div(tl, n_i), lax.rem(tl, n_i)

    def cur_map(t):
        b, i = cur(t)
        return b, i, 0

    def lag_map(t):
        b, i = lagged(t)
        return b, i, 0

    smem = pl.BlockSpec(memory_space=pltpu.SMEM)
    in_specs = [
        smem,
        smem,
        _resident((BLK, 3 * BLK)),
        _resident((6 * BLK, LANES)),
        pl.BlockSpec((1, tb, d), cur_map),
        pl.BlockSpec((1, tb, d), lag_map),
        _resident((SUBLANES, d)),
        _resident((d, IN_WIDTH)),
        _resident((SUBLANES, ATT_WIDTH)),
        _resident((SUBLANES, ATT_WIDTH)),
        _resident((SUBLANES, GM_WIDTH)),
        _resident((PAIRS, BLK, 2 * BLK)),
        _resident((PAIRS, BLK, LANES)),
        _resident((SUBLANES, ATT_WIDTH)),
        _resident((SUBLANES, GM_WIDTH)),
        _resident((d, d)),
    ]
    stage = [
        pltpu.VMEM((tb, ATT_WIDTH), jnp.bfloat16),
        pltpu.VMEM((2 * N_KV, tb + 2 * BLK, LANES), jnp.bfloat16),
        pltpu.VMEM((2 * N_KV, tb + 2 * BLK, LANES), jnp.bfloat16),
        pltpu.VMEM((tb, GM_WIDTH), jnp.bfloat16),
    ]
    scratch = [
        pltpu.VMEM((3, N_HEADS, BLK, 3 * BLK), jnp.float32),
        pltpu.VMEM((tb, ATT_WIDTH), jnp.float32),
        pltpu.VMEM((tb, GM_WIDTH), jnp.float32),
        pltpu.VMEM((tb, GM_WIDTH), jnp.float32),
    ] + stage + stage
    return pl.pallas_call(
        functools.partial(_mixer_kernel, blocks_per_seq=n_i),
        grid=(n_t + 1,),
        in_specs=in_specs,
        out_specs=pl.BlockSpec((1, tb, d), lag_map),
        out_shape=jax.ShapeDtypeStruct(x.shape, x.dtype),
        scratch_shapes=scratch,
        compiler_params=pltpu.CompilerParams(
            dimension_semantics=("arbitrary",),
            vmem_limit_bytes=VMEM_LIMIT_BYTES),
        name="mixer",
    )(p["table"], p["sink"], p["bucket"], p["indicator"], x, x, p["norm1"], p["w_in"], p["q_gain"],
      p["k_gain"], p["v_gain"], p["w_s"], p["b_s"], p["a_gain"], p["g_gain"], p["w_o"])


def _ffn(x2d, p):
    n, d = x2d.shape
    tm = FFN_TOKENS
    assert n % tm == 0
    d_ff = p["w_gate"].shape[1]
    return pl.pallas_call(
        _ffn_kernel,
        grid=(n // tm,),
        in_specs=[
            pl.BlockSpec((tm, d), lambda i: (i, 0)),
            _resident((SUBLANES, d)),
            _resident((d, d_ff)),
            _resident((d, d_ff)),
            _resident((d_ff, d)),
        ],
        out_specs=pl.BlockSpec((tm, d), lambda i: (i, 0)),
        out_shape=jax.ShapeDtypeStruct(x2d.shape, x2d.dtype),
        compiler_params=pltpu.CompilerParams(
            dimension_semantics=("arbitrary",),
            vmem_limit_bytes=VMEM_LIMIT_BYTES),
        name="ffn",
    )(x2d, p["norm2"], p["w_gate"], p["w_up"], p["w_down"])


def _layer_params(rel_bias_table, norm1, w_in, q_gain, k_gain, sink, v_gain, w_s, b_s,
                  attn_out_gain, gmlp_out_gain, w_o, norm2, w_gate, w_up, w_down):
    bf16 = jnp.bfloat16

    def row(v, width=None):
        v = v.reshape(1, -1).astype(jnp.float32)
        if width is not None:
            v = jnp.pad(v, ((0, 0), (0, width - v.shape[1])))
        return jnp.broadcast_to(v, (SUBLANES, v.shape[1]))

    w_pairs = jnp.concatenate([w_s[0::2], w_s[1::2]], axis=-1).astype(bf16)
    b_pairs = jnp.repeat(jnp.transpose(b_s.reshape(PAIRS, 2, BLK), (0, 2, 1)), HEAD_DIM, axis=-1)
    return {
        "table": rel_bias_table.astype(jnp.float32),
        "sink": sink.astype(jnp.float32),
        "bucket": jnp.asarray(_t5_bucket_table()),
        "indicator": jnp.asarray(_key_set_indicator(), dtype=bf16),
        "norm1": row(norm1),
        "w_in": w_in.astype(bf16),
        "q_gain": row(jnp.tile(q_gain, N_HEADS)),
        "k_gain": row(jnp.tile(k_gain, N_KV), ATT_WIDTH),
        "v_gain": row(v_gain),
        "w_s": w_pairs,
        "b_s": b_pairs.astype(jnp.float32),
        "a_gain": row(attn_out_gain),
        "g_gain": row(gmlp_out_gain),
        "w_o": w_o.astype(bf16),
        "norm2": row(norm2),
        "w_gate": w_gate.astype(bf16),
        "w_up": w_up.astype(bf16),
        "w_down": w_down.astype(bf16),
    }


def kernel(x_prompt, x_sample, rel_bias_table, norm1, w_in, q_gain, k_gain, sink, v_gain, w_s, b_s,
           attn_out_gain, gmlp_out_gain, w_o, norm2, w_gate, w_up, w_down):
    layers = [
        _layer_params(rel_bias_table, norm1[l], w_in[l], q_gain[l], k_gain[l], sink[l], v_gain[l],
                      w_s[l], b_s[l], attn_out_gain[l], gmlp_out_gain[l], w_o[l], norm2[l],
                      w_gate[l], w_up[l], w_down[l])
        for l in range(norm1.shape[0])]
    outs = []
    for x in (x_prompt, x_sample):
        for p in layers:
            x = _mixer(x, p)
            x = _ffn(x.reshape(-1, D_MODEL), p).reshape(x.shape)
        outs.append(x)
    return tuple(outs)
```

```python
import functools
import math

import jax
import jax.numpy as jnp
import numpy as np
from jax import lax
from jax.experimental import pallas as pl
from jax.experimental.pallas import tpu as pltpu

D_MODEL = 1024
HEAD_DIM = 64
N_HEADS = 8
N_KV = 2
ATT_WIDTH = N_HEADS * HEAD_DIM
KV_WIDTH = N_KV * HEAD_DIM
GM_HEADS = 8
GM_WIDTH = GM_HEADS * HEAD_DIM
IN_WIDTH = ATT_WIDTH + 2 * KV_WIDTH + 2 * GM_WIDTH
BLK = 128
N_BUCKETS = 32
MAX_DIST = 128
EPS = 1e-6
NEG = -1e30
LOG2_E = math.log2(math.e)

LANES = 128
SUBLANES = 8
PAIRS = ATT_WIDTH // LANES
MIX_TOKENS = 512
FFN_TOKENS = 512
VMEM_LIMIT_BYTES = 56 * 1024 * 1024

_Q_OFF = 0
_KV_OFF = ATT_WIDTH
_G_OFF = ATT_WIDTH + 2 * KV_WIDTH


def _t5_bucket_table():
    rel = (np.arange(3 * BLK) - BLK)[None, :] - np.arange(BLK)[:, None]
    half = N_BUCKETS // 2
    max_exact = half // 2
    ret = (rel > 0).astype(np.int32) * half
    n = np.abs(rel)
    large = max_exact + (np.log(np.maximum(n, 1).astype(np.float32) / max_exact)
                         / np.log(MAX_DIST / max_exact) * (half - max_exact)).astype(np.int32)
    large = np.minimum(large, half - 1)
    bucket = (ret + np.where(n < max_exact, n, large)).astype(np.int32)
    return np.where(np.abs(rel) <= BLK, bucket, -1).astype(np.int32)


def _key_set_indicator():
    key_is_low = np.arange(6 * BLK)[:, None] < 3 * BLK
    lane_is_low = np.arange(LANES)[None, :] < HEAD_DIM
    return (key_is_low == lane_is_low).astype(np.float32)


def _row_rms(x, gain):
    ms = jnp.mean(x * x, axis=-1, keepdims=True)
    return x * lax.rsqrt(ms + EPS) * gain


def _head_rms(x, gain, low_half):
    x2 = x * x
    s_lo = jnp.sum(jnp.where(low_half, x2, 0.0), axis=-1, keepdims=True)
    s_hi = jnp.sum(jnp.where(low_half, 0.0, x2), axis=-1, keepdims=True)
    r_lo = lax.rsqrt(s_lo * (1.0 / HEAD_DIM) + EPS)
    r_hi = lax.rsqrt(s_hi * (1.0 / HEAD_DIM) + EPS)
    return x * jnp.where(low_half, r_lo, r_hi) * gain


def _gelu_tanh(x):
    a = -2.0 * math.sqrt(2.0 / math.pi) * LOG2_E
    w = x * (x * x * (a * 0.044715) + a)
    return x * (1.0 / (1.0 + jnp.exp2(w)))


def _build_bias(table_ref, bucket_ref, bias_sc):
    f32 = jnp.float32
    neg = jnp.full((BLK, BLK), NEG, f32)
    for h in range(N_HEADS):
        for part in range(3):
            bk = bucket_ref[:, part * BLK:(part + 1) * BLK]

            def body(b, acc):
                return jnp.where(bk == b, table_ref[b, h] * LOG2_E, acc)

            acc = lax.fori_loop(0, N_BUCKETS, body, neg)
            sl = slice(part * BLK, (part + 1) * BLK)
            bias_sc[0, h, :, sl] = acc
            bias_sc[1, h, :, sl] = neg if part == 0 else acc
            bias_sc[2, h, :, sl] = neg if part == 2 else acc


def _front_pieces(xc_ref, norm1_ref, w_in_ref, q_gain_ref, k_gain_ref, v_gain_ref, w_s_ref,
                  b_s_ref, g_gain_ref, gu_sc, gv_sc, kz_prev, vz_prev, q_sc, kz_sc, vz_sc, g_sc):
    f32 = jnp.float32
    bf16 = jnp.bfloat16
    tb = xc_ref.shape[1]
    n_blk = tb // BLK
    low_half = lax.broadcasted_iota(jnp.int32, (1, LANES), 1) < HEAD_DIM
    wide = 2 * LANES
    state = {}

    def project(h, off):
        return jnp.dot(h, w_in_ref[:, off:off + wide], preferred_element_type=f32)

    def keys_values():
        hc = _row_rms(xc_ref[0], norm1_ref[:1, :]).astype(bf16)
        state["hc"] = hc
        kv = project(hc, _KV_OFF)
        kn = _head_rms(kv[:, :KV_WIDTH], k_gain_ref[:1, :KV_WIDTH], low_half)
        vv = kv[:, KV_WIDTH:]
        kn_sw = pltpu.roll(kn, HEAD_DIM, axis=1)
        vv_sw = pltpu.roll(vv, HEAD_DIM, axis=1)
        zero = jnp.zeros_like(kn)
        tail = slice(tb, tb + BLK)
        for idx, (src_k, src_v, keep_low) in enumerate(
                ((kn, vv, True), (kn_sw, vv_sw, False), (kn_sw, vv_sw, True), (kn, vv, False))):
            keep = low_half if keep_low else jnp.logical_not(low_half)
            for src, dst, dst_prev in ((src_k, kz_sc, kz_prev), (src_v, vz_sc, vz_prev)):
                laid = jnp.where(keep, src, zero).astype(bf16)
                dst[idx, BLK:BLK + tb, :] = laid
                dst_prev[idx, BLK + tb:, :] = laid[:BLK]
                dst[idx, :BLK, :] = dst_prev[idx, tail, :]

    def queries(tile):
        def run():
            qf = project(state["hc"], _Q_OFF + tile * wide)
            for cl in range(wide // LANES):
                sl = slice(tile * wide + cl * LANES, tile * wide + (cl + 1) * LANES)
                gain = q_gain_ref[:1, sl] * (LOG2_E / math.sqrt(HEAD_DIM))
                q_sc[:, sl] = _head_rms(qf[:, cl * LANES:(cl + 1) * LANES], gain,
                                        low_half).astype(bf16)
        return run

    def gelu_tile(tile):
        def run():
            gf = _gelu_tanh(project(state["hc"], _G_OFF + tile * wide))
            dst, col = (gu_sc, tile * wide) if tile * wide < GM_WIDTH else (
                gv_sc, tile * wide - GM_WIDTH)
            dst[:, col:col + wide] = gf
        return run

    def gate_rows(r):
        def run():
            rows = slice(r * BLK, (r + 1) * BLK)
            gv = _row_rms(gv_sc[rows, :], v_gain_ref[:1, :])
            tiles = []
            for c in range(PAIRS):
                sl = slice(c * LANES, (c + 1) * LANES)
                vt = gv[:, sl]
                stacked = jnp.concatenate(
                    [jnp.where(low_half, vt, 0.0), jnp.where(low_half, 0.0, vt)],
                    axis=0).astype(bf16)
                s = jnp.dot(w_s_ref[c], stacked, preferred_element_type=f32) + b_s_ref[c]
                tiles.append(gu_sc[rows, sl] * s)
            gm = jnp.concatenate(tiles, axis=1)
            g_sc[rows, :] = _row_rms(gm, g_gain_ref[:1, :]).astype(bf16)
        return run

    early = [keys_values] + [gelu_tile(tile) for tile in range(2 * GM_WIDTH // wide)]
    gates = [gate_rows(r) for r in range(n_blk)]
    queries_ = [queries(tile) for tile in range(ATT_WIDTH // wide)]
    return early, gates, queries_


def _back_pieces(x_ref, sink_ref, ind_ref, a_gain_ref, w_o_ref, bias_sc, q_sc, kz_sc, vz_sc, g_sc,
                 att_sc, out_ref, seq_first, seq_last):
    f32 = jnp.float32
    bf16 = jnp.bfloat16
    n_blk = x_ref.shape[1] // BLK
    low_half = lax.broadcasted_iota(jnp.int32, (1, LANES), 1) < HEAD_DIM
    tiles_per_kv = PAIRS // N_KV

    def attend(qb, j):
        r0 = qb * BLK
        tiles = [j * tiles_per_kv + cl for cl in range(tiles_per_kv)]
        band = slice(r0, r0 + 3 * BLK)
        held = {}

        def scores():
            if qb == 0:
                variant = jnp.where(seq_first, 1, 0)
            elif qb == n_blk - 1:
                variant = jnp.where(seq_last, 2, 0)
            else:
                variant = 0
            kz = jnp.concatenate([kz_sc[2 * j, band, :], kz_sc[2 * j + 1, band, :]], axis=0)
            qs = jnp.concatenate(
                [q_sc[r0:r0 + BLK, c * LANES:(c + 1) * LANES] for c in tiles], axis=0)
            s2 = lax.dot_general(qs, kz, (((1,), (1,)), ((), ())),
                                 preferred_element_type=f32)
            e_rows = []
            sink_rows = []
            for cl, c in enumerate(tiles):
                es = []
                sinks = []
                for p in range(2):
                    h = 2 * c + p
                    s = (s2[cl * BLK:(cl + 1) * BLK, p * 3 * BLK:(p + 1) * 3 * BLK]
                         + bias_sc[variant, h])
                    m = jnp.max(s, axis=-1, keepdims=True)
                    es.append(jnp.exp2(s - m).astype(bf16))
                    sinks.append(jnp.exp2(sink_ref[h] * LOG2_E - m))
                e_rows.append(jnp.concatenate(es, axis=1))
                sink_rows.append(jnp.where(low_half, sinks[0], sinks[1]))
            held["e"] = jnp.concatenate(e_rows, axis=0)
            held["sink"] = sink_rows

        def values():
            vz = jnp.concatenate([vz_sc[2 * j, band, :], vz_sc[2 * j + 1, band, :]], axis=0)
            pv = jnp.dot(held["e"], jnp.concatenate([vz, ind_ref[...]], axis=1),
                         preferred_element_type=f32)
            for cl, c in enumerate(tiles):
                rows = slice(cl * BLK, (cl + 1) * BLK)
                att_sc[r0:r0 + BLK, c * LANES:(c + 1) * LANES] = (
                    pv[rows, :LANES] / (pv[rows, LANES:] + held["sink"][cl]))

        return scores, values

    def project_out(half):
        def run():
            rows = slice(half * 2 * BLK, (half + 1) * 2 * BLK)
            a = _row_rms(att_sc[rows, :], a_gain_ref[:1, :]).astype(bf16)
            y = jnp.dot(a, w_o_ref[:ATT_WIDTH, :], preferred_element_type=f32)
            y = y + jnp.dot(g_sc[rows, :], w_o_ref[ATT_WIDTH:, :], preferred_element_type=f32)
            out_ref[0, rows, :] = x_ref[0, rows, :] + y
        return run

    attention = [attend(qb, j) for qb in range(n_blk) for j in range(N_KV)]
    outputs = [project_out(half) for half in range(n_blk // 2)]
    return attention, outputs


def _mixer_kernel(table_ref, sink_ref, bucket_ref, ind_ref, xc_ref, xl_ref, norm1_ref,
                  w_in_ref, q_gain_ref, k_gain_ref, v_gain_ref, w_s_ref, b_s_ref, a_gain_ref,
                  g_gain_ref, w_o_ref, out_ref, bias_sc, att_sc, gu_sc, gv_sc, *stage_sc,
                  blocks_per_seq):
    t = pl.program_id(0)
    set_a, set_b = stage_sc[:4], stage_sc[4:]

    @pl.when(t == 0)
    def _init():
        _build_bias(table_ref, bucket_ref, bias_sc)
        for ref in set_b:
            ref[...] = jnp.zeros(ref.shape, ref.dtype)

    lag = jnp.maximum(t - 1, 0)
    pos = lax.rem(lag, blocks_per_seq)
    seq_first = pos == 0
    seq_last = pos == blocks_per_seq - 1

    def run(write_set, read_set):
        early, gates, query_tiles = _front_pieces(
            xc_ref, norm1_ref, w_in_ref, q_gain_ref, k_gain_ref, v_gain_ref, w_s_ref, b_s_ref,
            g_gain_ref, gu_sc, gv_sc, read_set[1], read_set[2], *write_set)
        attention, outputs = _back_pieces(xl_ref, sink_ref, ind_ref, a_gain_ref, w_o_ref, bias_sc,
                                          *read_set, att_sc, out_ref, seq_first, seq_last)
        n_items = len(attention)
        between = early + [lambda: None] * (n_items - len(early))
        for i in range(n_items + 1):
            if i < n_items:
                attention[i][0]()
                between[i]()
            if i >= 1:
                attention[i - 1][1]()
        w_first, w_last = outputs
        q_first, q_last = query_tiles
        for piece in (w_first, gates[0], q_first, gates[1], q_last, *gates[2:], w_last):
            piece()

    parity = lax.rem(t, 2)

    @pl.when(parity == 0)
    def _even():
        run(set_a, set_b)

    @pl.when(parity == 1)
    def _odd():
        run(set_b, set_a)


def _ffn_kernel(x_ref, norm2_ref, w_gate_ref, w_up_ref, w_down_ref, out_ref):
    f32 = jnp.float32
    x = x_ref[...]
    h = _row_rms(x, norm2_ref[:1, :]).astype(jnp.bfloat16)
    gate = jnp.dot(h, w_gate_ref[...], preferred_element_type=f32)
    up = jnp.dot(h, w_up_ref[...], preferred_element_type=f32)
    act = (gate * (1.0 / (1.0 + jnp.exp(-gate))) * up).astype(jnp.bfloat16)
    out_ref[...] = x + jnp.dot(act, w_down_ref[...], preferred_element_type=f32)


def _resident(shape):
    zeros = (0,) * len(shape)
    return pl.BlockSpec(shape, lambda *_: zeros, pipeline_mode=pl.Buffered(1))


def _mixer(x, p):
    batch, seq, d = x.shape
    tb = MIX_TOKENS
    assert seq % tb == 0 and tb % (2 * BLK) == 0 and seq // BLK >= 2 and d == D_MODEL
    n_i = seq // tb
    n_t = batch * n_i

    def cur(t):
        tc = jnp.minimum(t, n_t - 1)
        return lax.div(tc, n_i), lax.rem(tc, n_i)

    def lagged(t):
        tl = jnp.maximum(t - 1, 0)
        return lax.div(tl, n_i), lax.rem(tl, n_i)

    def cur_map(t):
        b, i = cur(t)
        return b, i, 0

    def lag_map(t):
        b, i = lagged(t)
        return b, i, 0

    smem = pl.BlockSpec(memory_space=pltpu.SMEM)
    in_specs = [
        smem,
        smem,
        _resident((BLK, 3 * BLK)),
        _resident((6 * BLK, LANES)),
        pl.BlockSpec((1, tb, d), cur_map),
        pl.BlockSpec((1, tb, d), lag_map),
        _resident((SUBLANES, d)),
        _resident((d, IN_WIDTH)),
        _resident((SUBLANES, ATT_WIDTH)),
        _resident((SUBLANES, ATT_WIDTH)),
        _resident((SUBLANES, GM_WIDTH)),
        _resident((PAIRS, BLK, 2 * BLK)),
        _resident((PAIRS, BLK, LANES)),
        _resident((SUBLANES, ATT_WIDTH)),
        _resident((SUBLANES, GM_WIDTH)),
        _resident((d, d)),
    ]
    stage = [
        pltpu.VMEM((tb, ATT_WIDTH), jnp.bfloat16),
        pltpu.VMEM((2 * N_KV, tb + 2 * BLK, LANES), jnp.bfloat16),
        pltpu.VMEM((2 * N_KV, tb + 2 * BLK, LANES), jnp.bfloat16),
        pltpu.VMEM((tb, GM_WIDTH), jnp.bfloat16),
    ]
    scratch = [
        pltpu.VMEM((3, N_HEADS, BLK, 3 * BLK), jnp.float32),
        pltpu.VMEM((tb, ATT_WIDTH), jnp.float32),
        pltpu.VMEM((tb, GM_WIDTH), jnp.float32),
        pltpu.VMEM((tb, GM_WIDTH), jnp.float32),
    ] + stage + stage
    return pl.pallas_call(
        functools.partial(_mixer_kernel, blocks_per_seq=n_i),
        grid=(n_t + 1,),
        in_specs=in_specs,
        out_specs=pl.BlockSpec((1, tb, d), lag_map),
        out_shape=jax.ShapeDtypeStruct(x.shape, x.dtype),
        scratch_shapes=scratch,
        compiler_params=pltpu.CompilerParams(
            dimension_semantics=("arbitrary",),
            vmem_limit_bytes=VMEM_LIMIT_BYTES),
        name="mixer",
    )(p["table"], p["sink"], p["bucket"], p["indicator"], x, x, p["norm1"], p["w_in"], p["q_gain"],
      p["k_gain"], p["v_gain"], p["w_s"], p["b_s"], p["a_gain"], p["g_gain"], p["w_o"])


def _ffn(x2d, p):
    n, d = x2d.shape
    tm = FFN_TOKENS
    assert n % tm == 0
    d_ff = p["w_gate"].shape[1]
    return pl.pallas_call(
        _ffn_kernel,
        grid=(n // tm,),
        in_specs=[
            pl.BlockSpec((tm, d), lambda i: (i, 0)),
            _resident((SUBLANES, d)),
            _resident((d, d_ff)),
            _resident((d, d_ff)),
            _resident((d_ff, d)),
        ],
        out_specs=pl.BlockSpec((tm, d), lambda i: (i, 0)),
        out_shape=jax.ShapeDtypeStruct(x2d.shape, x2d.dtype),
        compiler_params=pltpu.CompilerParams(
            dimension_semantics=("arbitrary",),
            vmem_limit_bytes=VMEM_LIMIT_BYTES),
        name="ffn",
    )(x2d, p["norm2"], p["w_gate"], p["w_up"], p["w_down"])


def _layer_params(rel_bias_table, norm1, w_in, q_gain, k_gain, sink, v_gain, w_s, b_s,
                  attn_out_gain, gmlp_out_gain, w_o, norm2, w_gate, w_up, w_down):
    bf16 = jnp.bfloat16

    def row(v, width=None):
        v = v.reshape(1, -1).astype(jnp.float32)
        if width is not None:
            v = jnp.pad(v, ((0, 0), (0, width - v.shape[1])))
        return jnp.broadcast_to(v, (SUBLANES, v.shape[1]))

    w_pairs = jnp.concatenate([w_s[0::2], w_s[1::2]], axis=-1).astype(bf16)
    b_pairs = jnp.repeat(jnp.transpose(b_s.reshape(PAIRS, 2, BLK), (0, 2, 1)), HEAD_DIM, axis=-1)
    return {
        "table": rel_bias_table.astype(jnp.float32),
        "sink": sink.astype(jnp.float32),
        "bucket": jnp.asarray(_t5_bucket_table()),
        "indicator": jnp.asarray(_key_set_indicator(), dtype=bf16),
        "norm1": row(norm1),
        "w_in": w_in.astype(bf16),
        "q_gain": row(jnp.tile(q_gain, N_HEADS)),
        "k_gain": row(jnp.tile(k_gain, N_KV), ATT_WIDTH),
        "v_gain": row(v_gain),
        "w_s": w_pairs,
        "b_s": b_pairs.astype(jnp.float32),
        "a_gain": row(attn_out_gain),
        "g_gain": row(gmlp_out_gain),
        "w_o": w_o.astype(bf16),
        "norm2": row(norm2),
        "w_gate": w_gate.astype(bf16),
        "w_up": w_up.astype(bf16),
        "w_down": w_down.astype(bf16),
    }


def kernel(x_prompt, x_sample, rel_bias_table, norm1, w_in, q_gain, k_gain, sink, v_gain, w_s, b_s,
           attn_out_gain, gmlp_out_gain, w_o, norm2, w_gate, w_up, w_down):
    layers = [
        _layer_params(rel_bias_table, norm1[l], w_in[l], q_gain[l], k_gain[l], sink[l], v_gain[l],
                      w_s[l], b_s[l], attn_out_gain[l], gmlp_out_gain[l], w_o[l], norm2[l],
                      w_gate[l], w_up[l], w_down[l])
        for l in range(norm1.shape[0])]
    outs = []
    for x in (x_prompt, x_sample):
        for p in layers:
            x = _mixer(x, p)
            x = _ffn(x.reshape(-1, D_MODEL), p).reshape(x.shape)
        outs.append(x)
    return tuple(outs)
```

```python
import functools
import math

import jax
import jax.numpy as jnp
import numpy as np
from jax import lax
from jax.experimental import pallas as pl
from jax.experimental.pallas import tpu as pltpu

D_MODEL = 1024
HEAD_DIM = 64
N_HEADS = 8
N_KV = 2
ATT_WIDTH = N_HEADS * HEAD_DIM
KV_WIDTH = N_KV * HEAD_DIM
GM_HEADS = 8
GM_WIDTH = GM_HEADS * HEAD_DIM
IN_WIDTH = ATT_WIDTH + 2 * KV_WIDTH + 2 * GM_WIDTH
BLK = 128
N_BUCKETS = 32
MAX_DIST = 128
EPS = 1e-6
NEG = -1e30
LOG2_E = math.log2(math.e)

LANES = 128
SUBLANES = 8
PAIRS = ATT_WIDTH // LANES
MIX_TOKENS = 512
FFN_TOKENS = 512
VMEM_LIMIT_BYTES = 56 * 1024 * 1024

_Q_OFF = 0
_KV_OFF = ATT_WIDTH
_G_OFF = ATT_WIDTH + 2 * KV_WIDTH


def _t5_bucket_table():
    rel = (np.arange(3 * BLK) - BLK)[None, :] - np.arange(BLK)[:, None]
    half = N_BUCKETS // 2
    max_exact = half // 2
    ret = (rel > 0).astype(np.int32) * half
    n = np.abs(rel)
    large = max_exact + (np.log(np.maximum(n, 1).astype(np.float32) / max_exact)
                         / np.log(MAX_DIST / max_exact) * (half - max_exact)).astype(np.int32)
    large = np.minimum(large, half - 1)
    bucket = (ret + np.where(n < max_exact, n, large)).astype(np.int32)
    return np.where(np.abs(rel) <= BLK, bucket, -1).astype(np.int32)


def _key_set_indicator():
    key_is_low = np.arange(6 * BLK)[:, None] < 3 * BLK
    lane_is_low = np.arange(LANES)[None, :] < HEAD_DIM
    return (key_is_low == lane_is_low).astype(np.float32)


def _row_rms(x, gain):
    ms = jnp.mean(x * x, axis=-1, keepdims=True)
    return x * lax.rsqrt(ms + EPS) * gain


def _head_rms(x, gain, low_half):
    x2 = x * x
    s_lo = jnp.sum(jnp.where(low_half, x2, 0.0), axis=-1, keepdims=True)
    s_hi = jnp.sum(jnp.where(low_half, 0.0, x2), axis=-1, keepdims=True)
    r_lo = lax.rsqrt(s_lo * (1.0 / HEAD_DIM) + EPS)
    r_hi = lax.rsqrt(s_hi * (1.0 / HEAD_DIM) + EPS)
    return x * jnp.where(low_half, r_lo, r_hi) * gain


def _gelu_tanh(x):
    a = -2.0 * math.sqrt(2.0 / math.pi) * LOG2_E
    w = x * (x * x * (a * 0.044715) + a)
    return x * (1.0 / (1.0 + jnp.exp2(w)))


def _build_bias(table_ref, bucket_ref, bias_sc):
    f32 = jnp.float32
    neg = jnp.full((BLK, BLK), NEG, f32)
    for h in range(N_HEADS):
        for part in range(3):
            bk = bucket_ref[:, part * BLK:(part + 1) * BLK]

            def body(b, acc):
                return jnp.where(bk == b, table_ref[b, h] * LOG2_E, acc)

            acc = lax.fori_loop(0, N_BUCKETS, body, neg)
            sl = slice(part * BLK, (part + 1) * BLK)
            bias_sc[0, h, :, sl] = acc
            bias_sc[1, h, :, sl] = neg if part == 0 else acc
            bias_sc[2, h, :, sl] = neg if part == 2 else acc


def _front_pieces(xc_ref, norm1_ref, w_in_ref, q_gain_ref, k_gain_ref, v_gain_ref, w_s_ref,
                  b_s_ref, g_gain_ref, gu_sc, gv_sc, kz_prev, vz_prev, q_sc, kz_sc, vz_sc, g_sc):
    f32 = jnp.float32
    bf16 = jnp.bfloat16
    tb = xc_ref.shape[1]
    n_blk = tb // BLK
    low_half = lax.broadcasted_iota(jnp.int32, (1, LANES), 1) < HEAD_DIM
    wide = 2 * LANES
    state = {}

    def project(h, off):
        return jnp.dot(h, w_in_ref[:, off:off + wide], preferred_element_type=f32)

    def keys_values():
        hc = _row_rms(xc_ref[0], norm1_ref[:1, :]).astype(bf16)
        state["hc"] = hc
        kv = project(hc, _KV_OFF)
        kn = _head_rms(kv[:, :KV_WIDTH], k_gain_ref[:1, :KV_WIDTH], low_half)
        vv = kv[:, KV_WIDTH:]
        kn_sw = pltpu.roll(kn, HEAD_DIM, axis=1)
        vv_sw = pltpu.roll(vv, HEAD_DIM, axis=1)
        zero = jnp.zeros_like(kn)
        tail = slice(tb, tb + BLK)
        for idx, (src_k, src_v, keep_low) in enumerate(
                ((kn, vv, True), (kn_sw, vv_sw, False), (kn_sw, vv_sw, True), (kn, vv, False))):
            keep = low_half if keep_low else jnp.logical_not(low_half)
            for src, dst, dst_prev in ((src_k, kz_sc, kz_prev), (src_v, vz_sc, vz_prev)):
                laid = jnp.where(keep, src, zero).astype(bf16)
                dst[idx, BLK:BLK + tb, :] = laid
                dst_prev[idx, BLK + tb:, :] = laid[:BLK]
                dst[idx, :BLK, :] = dst_prev[idx, tail, :]

    def queries(tile):
        def run():
            qf = project(state["hc"], _Q_OFF + tile * wide)
            for cl in range(wide // LANES):
                sl = slice(tile * wide + cl * LANES, tile * wide + (cl + 1) * LANES)
                gain = q_gain_ref[:1, sl] * (LOG2_E / math.sqrt(HEAD_DIM))
                q_sc[:, sl] = _head_rms(qf[:, cl * LANES:(cl + 1) * LANES], gain,
                                        low_half).astype(bf16)
        return run

    def gelu_tile(tile):
        def run():
            gf = _gelu_tanh(project(state["hc"], _G_OFF + tile * wide))
            dst, col = (gu_sc, tile * wide) if tile * wide < GM_WIDTH else (
                gv_sc, tile * wide - GM_WIDTH)
            dst[:, col:col + wide] = gf
        return run

    def gate_rows(r):
        def run():
            rows = slice(r * BLK, (r + 1) * BLK)
            gv = _row_rms(gv_sc[rows, :], v_gain_ref[:1, :])
            tiles = []
            for c in range(PAIRS):
                sl = slice(c * LANES, (c + 1) * LANES)
                vt = gv[:, sl]
                stacked = jnp.concatenate(
                    [jnp.where(low_half, vt, 0.0), jnp.where(low_half, 0.0, vt)],
                    axis=0).astype(bf16)
                s = jnp.dot(w_s_ref[c], stacked, preferred_element_type=f32) + b_s_ref[c]
                tiles.append(gu_sc[rows, sl] * s)
            gm = jnp.concatenate(tiles, axis=1)
            g_sc[rows, :] = _row_rms(gm, g_gain_ref[:1, :]).astype(bf16)
        return run

    early = [keys_values] + [gelu_tile(tile) for tile in range(2 * GM_WIDTH // wide)]
    gates = [gate_rows(r) for r in range(n_blk)]
    queries_ = [queries(tile) for tile in range(ATT_WIDTH // wide)]
    return early, gates, queries_


def _back_pieces(x_ref, sink_ref, ind_ref, a_gain_ref, w_o_ref, bias_sc, q_sc, kz_sc, vz_sc, g_sc,
                 att_sc, out_ref, seq_first, seq_last):
    f32 = jnp.float32
    bf16 = jnp.bfloat16
    n_blk = x_ref.shape[1] // BLK
    low_half = lax.broadcasted_iota(jnp.int32, (1, LANES), 1) < HEAD_DIM
    tiles_per_kv = PAIRS // N_KV

    def attend(qb, j):
        r0 = qb * BLK
        tiles = [j * tiles_per_kv + cl for cl in range(tiles_per_kv)]
        band = slice(r0, r0 + 3 * BLK)
        held = {}

        def scores():
            if qb == 0:
                variant = jnp.where(seq_first, 1, 0)
            elif qb == n_blk - 1:
                variant = jnp.where(seq_last, 2, 0)
            else:
                variant = 0
            kz = jnp.concatenate([kz_sc[2 * j, band, :], kz_sc[2 * j + 1, band, :]], axis=0)
            qs = jnp.concatenate(
                [q_sc[r0:r0 + BLK, c * LANES:(c + 1) * LANES] for c in tiles], axis=0)
            s2 = lax.dot_general(qs, kz, (((1,), (1,)), ((), ())),
                                 preferred_element_type=f32)
            e_rows = []
            sink_rows = []
            for cl, c in enumerate(tiles):
                es = []
                sinks = []
                for p in range(2):
                    h = 2 * c + p
                    s = (s2[cl * BLK:(cl + 1) * BLK, p * 3 * BLK:(p + 1) * 3 * BLK]
                         + bias_sc[variant, h])
                    m = jnp.max(s, axis=-1, keepdims=True)
                    es.append(jnp.exp2(s - m).astype(bf16))
                    sinks.append(jnp.exp2(sink_ref[h] * LOG2_E - m))
                e_rows.append(jnp.concatenate(es, axis=1))
                sink_rows.append(jnp.where(low_half, sinks[0], sinks[1]))
            held["e"] = jnp.concatenate(e_rows, axis=0)
            held["sink"] = sink_rows

        def values():
            vz = jnp.concatenate([vz_sc[2 * j, band, :], vz_sc[2 * j + 1, band, :]], axis=0)
            pv = jnp.dot(held["e"], jnp.concatenate([vz, ind_ref[...]], axis=1),
                         preferred_element_type=f32)
            for cl, c in enumerate(tiles):
                rows = slice(cl * BLK, (cl + 1) * BLK)
                att_sc[r0:r0 + BLK, c * LANES:(c + 1) * LANES] = (
                    pv[rows, :LANES] / (pv[rows, LANES:] + held["sink"][cl]))

        return scores, values

    def project_out(half):
        def run():
            rows = slice(half * 2 * BLK, (half + 1) * 2 * BLK)
            a = _row_rms(att_sc[rows, :], a_gain_ref[:1, :]).astype(bf16)
            y = jnp.dot(a, w_o_ref[:ATT_WIDTH, :], preferred_element_type=f32)
            y = y + jnp.dot(g_sc[rows, :], w_o_ref[ATT_WIDTH:, :], preferred_element_type=f32)
            out_ref[0, rows, :] = x_ref[0, rows, :] + y
        return run

    attention = [attend(qb, j) for qb in range(n_blk) for j in range(N_KV)]
    outputs = [project_out(half) for half in range(n_blk // 2)]
    return attention, outputs


def _mixer_kernel(table_ref, sink_ref, bucket_ref, ind_ref, xc_ref, xl_ref, norm1_ref,
                  w_in_ref, q_gain_ref, k_gain_ref, v_gain_ref, w_s_ref, b_s_ref, a_gain_ref,
                  g_gain_ref, w_o_ref, out_ref, bias_sc, att_sc, gu_sc, gv_sc, *stage_sc,
                  blocks_per_seq, n_steps):
    t = pl.program_id(0)
    last = n_steps - 1
    set_a, set_b = stage_sc[:4], stage_sc[4:]

    lag = jnp.maximum(t - 1, 0)
    pos = lax.rem(lag, blocks_per_seq)
    seq_first = pos == 0
    seq_last = pos == blocks_per_seq - 1

    def pieces(write_set, read_set):
        front = _front_pieces(
            xc_ref, norm1_ref, w_in_ref, q_gain_ref, k_gain_ref, v_gain_ref, w_s_ref, b_s_ref,
            g_gain_ref, gu_sc, gv_sc, read_set[1], read_set[2], *write_set)
        back = _back_pieces(xl_ref, sink_ref, ind_ref, a_gain_ref, w_o_ref, bias_sc,
                            *read_set, att_sc, out_ref, seq_first, seq_last)
        return front, back

    @pl.when(t == 0)
    def _first():
        _build_bias(table_ref, bucket_ref, bias_sc)
        for ref in set_b[1:3]:
            ref[...] = jnp.zeros(ref.shape, ref.dtype)
        (early, gates, query_tiles), _ = pieces(set_a, set_b)
        for piece in early + query_tiles + gates:
            piece()

    @pl.when(t == last)
    def _drain():
        read_set = set_b if last % 2 == 0 else set_a
        write_set = set_a if last % 2 == 0 else set_b
        _, (attention, outputs) = pieces(write_set, read_set)
        per_half = len(attention) // len(outputs)
        for i, (scores, values) in enumerate(attention):
            scores()
            values()
            if (i + 1) % per_half == 0:
                outputs[i // per_half]()

    def run(write_set, read_set):
        (early, gates, query_tiles), (attention, outputs) = pieces(write_set, read_set)
        n_items = len(attention)
        skip = lambda: None
        between = [early[0], early[1], skip, early[2], skip, early[3], skip, early[4]]
        for i in range(n_items + 1):
            if i < n_items:
                attention[i][0]()
                between[i]()
            if i >= 1:
                attention[i - 1][1]()
        w_first, w_last = outputs
        q_first, q_last = query_tiles
        for piece in (w_first, gates[0], q_first, gates[1], q_last, *gates[2:], w_last):
            piece()

    steady = jnp.logical_and(t > 0, t < last)
    parity = lax.rem(t, 2)

    @pl.when(jnp.logical_and(steady, parity == 0))
    def _even():
        run(set_a, set_b)

    @pl.when(jnp.logical_and(steady, parity == 1))
    def _odd():
        run(set_b, set_a)


def _ffn_kernel(x_ref, norm2_ref, w_gate_ref, w_up_ref, w_down_ref, out_ref):
    f32 = jnp.float32
    x = x_ref[...]
    h = _row_rms(x, norm2_ref[:1, :]).astype(jnp.bfloat16)
    gate = jnp.dot(h, w_gate_ref[...], preferred_element_type=f32)
    up = jnp.dot(h, w_up_ref[...], preferred_element_type=f32)
    act = (gate * (1.0 / (1.0 + jnp.exp(-gate))) * up).astype(jnp.bfloat16)
    out_ref[...] = x + jnp.dot(act, w_down_ref[...], preferred_element_type=f32)


def _resident(shape):
    zeros = (0,) * len(shape)
    return pl.BlockSpec(shape, lambda *_: zeros, pipeline_mode=pl.Buffered(1))


def _mixer(x, p):
    batch, seq, d = x.shape
    tb = MIX_TOKENS
    assert seq % tb == 0 and tb % (2 * BLK) == 0 and seq // BLK >= 2 and d == D_MODEL
    n_i = seq // tb
    n_t = batch * n_i

    def cur(t):
        tc = jnp.minimum(t, n_t - 1)
        return lax.div(tc, n_i), lax.rem(tc, n_i)

    def lagged(t):
        tl = jnp.maximum(t - 1, 0)
        return lax.div(tl, n_i), lax.rem(tl, n_i)

    def cur_map(t):
        b, i = cur(t)
        return b, i, 0

    def lag_map(t):
        b, i = lagged(t)
        return b, i, 0

    smem = pl.BlockSpec(memory_space=pltpu.SMEM)
    in_specs = [
        smem,
        smem,
        _resident((BLK, 3 * BLK)),
        _resident((6 * BLK, LANES)),
        pl.BlockSpec((1, tb, d), cur_map),
        pl.BlockSpec((1, tb, d), lag_map),
        _resident((SUBLANES, d)),
        _resident((d, IN_WIDTH)),
        _resident((SUBLANES, ATT_WIDTH)),
        _resident((SUBLANES, ATT_WIDTH)),
        _resident((SUBLANES, GM_WIDTH)),
        _resident((PAIRS, BLK, 2 * BLK)),
        _resident((PAIRS, BLK, LANES)),
        _resident((SUBLANES, ATT_WIDTH)),
        _resident((SUBLANES, GM_WIDTH)),
        _resident((d, d)),
    ]
    stage = [
        pltpu.VMEM((tb, ATT_WIDTH), jnp.bfloat16),
        pltpu.VMEM((2 * N_KV, tb + 2 * BLK, LANES), jnp.bfloat16),
        pltpu.VMEM((2 * N_KV, tb + 2 * BLK, LANES), jnp.bfloat16),
        pltpu.VMEM((tb, GM_WIDTH), jnp.bfloat16),
    ]
    scratch = [
        pltpu.VMEM((3, N_HEADS, BLK, 3 * BLK), jnp.float32),
        pltpu.VMEM((tb, ATT_WIDTH), jnp.float32),
        pltpu.VMEM((tb, GM_WIDTH), jnp.float32),
        pltpu.VMEM((tb, GM_WIDTH), jnp.float32),
    ] + stage + stage
    return pl.pallas_call(
        functools.partial(_mixer_kernel, blocks_per_seq=n_i, n_steps=n_t + 1),
        grid=(n_t + 1,),
        in_specs=in_specs,
        out_specs=pl.BlockSpec((1, tb, d), lag_map),
        out_shape=jax.ShapeDtypeStruct(x.shape, x.dtype),
        scratch_shapes=scratch,
        compiler_params=pltpu.CompilerParams(
            dimension_semantics=("arbitrary",),
            vmem_limit_bytes=VMEM_LIMIT_BYTES),
        name="mixer",
    )(p["table"], p["sink"], p["bucket"], p["indicator"], x, x, p["norm1"], p["w_in"], p["q_gain"],
      p["k_gain"], p["v_gain"], p["w_s"], p["b_s"], p["a_gain"], p["g_gain"], p["w_o"])


def _ffn(x2d, p):
    n, d = x2d.shape
    tm = FFN_TOKENS
    assert n % tm == 0
    d_ff = p["w_gate"].shape[1]
    return pl.pallas_call(
        _ffn_kernel,
        grid=(n // tm,),
        in_specs=[
            pl.BlockSpec((tm, d), lambda i: (i, 0)),
            _resident((SUBLANES, d)),
            _resident((d, d_ff)),
            _resident((d, d_ff)),
            _resident((d_ff, d)),
        ],
        out_specs=pl.BlockSpec((tm, d), lambda i: (i, 0)),
        out_shape=jax.ShapeDtypeStruct(x2d.shape, x2d.dtype),
        compiler_params=pltpu.CompilerParams(
            dimension_semantics=("arbitrary",),
            vmem_limit_bytes=VMEM_LIMIT_BYTES),
        name="ffn",
    )(x2d, p["norm2"], p["w_gate"], p["w_up"], p["w_down"])


def _layer_params(rel_bias_table, norm1, w_in, q_gain, k_gain, sink, v_gain, w_s, b_s,
                  attn_out_gain, gmlp_out_gain, w_o, norm2, w_gate, w_up, w_down):
    bf16 = jnp.bfloat16

    def row(v, width=None):
        v = v.reshape(1, -1).astype(jnp.float32)
        if width is not None:
            v = jnp.pad(v, ((0, 0), (0, width - v.shape[1])))
        return jnp.broadcast_to(v, (SUBLANES, v.shape[1]))

    w_pairs = jnp.concatenate([w_s[0::2], w_s[1::2]], axis=-1).astype(bf16)
    b_pairs = jnp.repeat(jnp.transpose(b_s.reshape(PAIRS, 2, BLK), (0, 2, 1)), HEAD_DIM, axis=-1)
    return {
        "table": rel_bias_table.astype(jnp.float32),
        "sink": sink.astype(jnp.float32),
        "bucket": jnp.asarray(_t5_bucket_table()),
        "indicator": jnp.asarray(_key_set_indicator(), dtype=bf16),
        "norm1": row(norm1),
        "w_in": w_in.astype(bf16),
        "q_gain": row(jnp.tile(q_gain, N_HEADS)),
        "k_gain": row(jnp.tile(k_gain, N_KV), ATT_WIDTH),
        "v_gain": row(v_gain),
        "w_s": w_pairs,
        "b_s": b_pairs.astype(jnp.float32),
        "a_gain": row(attn_out_gain),
        "g_gain": row(gmlp_out_gain),
        "w_o": w_o.astype(bf16),
        "norm2": row(norm2),
        "w_gate": w_gate.astype(bf16),
        "w_up": w_up.astype(bf16),
        "w_down": w_down.astype(bf16),
    }


def kernel(x_prompt, x_sample, rel_bias_table, norm1, w_in, q_gain, k_gain, sink, v_gain, w_s, b_s,
           attn_out_gain, gmlp_out_gain, w_o, norm2, w_gate, w_up, w_down):
    layers = [
        _layer_params(rel_bias_table, norm1[l], w_in[l], q_gain[l], k_gain[l], sink[l], v_gain[l],
                      w_s[l], b_s[l], attn_out_gain[l], gmlp_out_gain[l], w_o[l], norm2[l],
                      w_gate[l], w_up[l], w_down[l])
        for l in range(norm1.shape[0])]
    outs = []
    for x in (x_prompt, x_sample):
        for p in layers:
            x = _mixer(x, p)
            x = _ffn(x.reshape(-1, D_MODEL), p).reshape(x.shape)
        outs.append(x)
    return tuple(outs)
```

```python
import functools
import math

import jax
import jax.numpy as jnp
import numpy as np
from jax import lax
from jax.experimental import pallas as pl
from jax.experimental.pallas import tpu as pltpu

D_MODEL = 1024
HEAD_DIM = 64
N_HEADS = 8
N_KV = 2
ATT_WIDTH = N_HEADS * HEAD_DIM
KV_WIDTH = N_KV * HEAD_DIM
GM_HEADS = 8
GM_WIDTH = GM_HEADS * HEAD_DIM
IN_WIDTH = ATT_WIDTH + 2 * KV_WIDTH + 2 * GM_WIDTH
BLK = 128
N_BUCKETS = 32
MAX_DIST = 128
EPS = 1e-6
NEG = -1e30
LOG2_E = math.log2(math.e)

LANES = 128
SUBLANES = 8
PAIRS = ATT_WIDTH // LANES
MIX_TOKENS = 512
FFN_TOKENS = 512
VMEM_LIMIT_BYTES = 56 * 1024 * 1024

_Q_OFF = 0
_KV_OFF = ATT_WIDTH
_G_OFF = ATT_WIDTH + 2 * KV_WIDTH


def _t5_bucket_table():
    rel = (np.arange(3 * BLK) - BLK)[None, :] - np.arange(BLK)[:, None]
    half = N_BUCKETS // 2
    max_exact = half // 2
    ret = (rel > 0).astype(np.int32) * half
    n = np.abs(rel)
    large = max_exact + (np.log(np.maximum(n, 1).astype(np.float32) / max_exact)
                         / np.log(MAX_DIST / max_exact) * (half - max_exact)).astype(np.int32)
    large = np.minimum(large, half - 1)
    bucket = (ret + np.where(n < max_exact, n, large)).astype(np.int32)
    return np.where(np.abs(rel) <= BLK, bucket, -1).astype(np.int32)


def _key_set_indicator():
    key_is_low = np.arange(6 * BLK)[:, None] < 3 * BLK
    lane_is_low = np.arange(LANES)[None, :] < HEAD_DIM
    return (key_is_low == lane_is_low).astype(np.float32)


def _row_rms(x, gain):
    ms = jnp.mean(x * x, axis=-1, keepdims=True)
    return x * lax.rsqrt(ms + EPS) * gain


def _head_rms(x, gain, low_half):
    x2 = x * x
    s_lo = jnp.sum(jnp.where(low_half, x2, 0.0), axis=-1, keepdims=True)
    s_hi = jnp.sum(jnp.where(low_half, 0.0, x2), axis=-1, keepdims=True)
    r_lo = lax.rsqrt(s_lo * (1.0 / HEAD_DIM) + EPS)
    r_hi = lax.rsqrt(s_hi * (1.0 / HEAD_DIM) + EPS)
    return x * jnp.where(low_half, r_lo, r_hi) * gain


def _gelu_tanh(x):
    a = -2.0 * math.sqrt(2.0 / math.pi) * LOG2_E
    w = x * (x * x * (a * 0.044715) + a)
    return x * (1.0 / (1.0 + jnp.exp2(w)))


def _build_bias(table_ref, bucket_ref, bias_sc):
    f32 = jnp.float32
    neg = jnp.full((BLK, BLK), NEG, f32)
    for h in range(N_HEADS):
        for part in range(3):
            bk = bucket_ref[:, part * BLK:(part + 1) * BLK]

            def body(b, acc):
                return jnp.where(bk == b, table_ref[b, h] * LOG2_E, acc)

            acc = lax.fori_loop(0, N_BUCKETS, body, neg)
            sl = slice(part * BLK, (part + 1) * BLK)
            bias_sc[0, h, :, sl] = acc
            bias_sc[1, h, :, sl] = neg if part == 0 else acc
            bias_sc[2, h, :, sl] = neg if part == 2 else acc


def _front_pieces(xc_ref, norm1_ref, w_in_ref, q_gain_ref, k_gain_ref, v_gain_ref, w_s_ref,
                  b_s_ref, g_gain_ref, gu_sc, gv_sc, kz_prev, vz_prev, q_sc, kz_sc, vz_sc, g_sc):
    f32 = jnp.float32
    bf16 = jnp.bfloat16
    tb = xc_ref.shape[1]
    n_blk = tb // BLK
    low_half = lax.broadcasted_iota(jnp.int32, (1, LANES), 1) < HEAD_DIM
    wide = 2 * LANES
    state = {}

    def project(h, off):
        return jnp.dot(h, w_in_ref[:, off:off + wide], preferred_element_type=f32)

    def keys_values():
        hc = _row_rms(xc_ref[0], norm1_ref[:1, :]).astype(bf16)
        state["hc"] = hc
        kv = project(hc, _KV_OFF)
        kn = _head_rms(kv[:, :KV_WIDTH], k_gain_ref[:1, :KV_WIDTH], low_half)
        vv = kv[:, KV_WIDTH:]
        kn_sw = pltpu.roll(kn, HEAD_DIM, axis=1)
        vv_sw = pltpu.roll(vv, HEAD_DIM, axis=1)
        zero = jnp.zeros_like(kn)
        tail = slice(tb, tb + BLK)
        for idx, (src_k, src_v, keep_low) in enumerate(
                ((kn, vv, True), (kn_sw, vv_sw, False), (kn_sw, vv_sw, True), (kn, vv, False))):
            keep = low_half if keep_low else jnp.logical_not(low_half)
            for src, dst, dst_prev in ((src_k, kz_sc, kz_prev), (src_v, vz_sc, vz_prev)):
                laid = jnp.where(keep, src, zero).astype(bf16)
                dst[idx, BLK:BLK + tb, :] = laid
                dst_prev[idx, BLK + tb:, :] = laid[:BLK]
                dst[idx, :BLK, :] = dst_prev[idx, tail, :]

    def queries(tile):
        def run():
            qf = project(state["hc"], _Q_OFF + tile * wide)
            for cl in range(wide // LANES):
                sl = slice(tile * wide + cl * LANES, tile * wide + (cl + 1) * LANES)
                gain = q_gain_ref[:1, sl] * (LOG2_E / math.sqrt(HEAD_DIM))
                q_sc[:, sl] = _head_rms(qf[:, cl * LANES:(cl + 1) * LANES], gain,
                                        low_half).astype(bf16)
        return run

    def gelu_tile(tile):
        def run():
            gf = _gelu_tanh(project(state["hc"], _G_OFF + tile * wide))
            dst, col = (gu_sc, tile * wide) if tile * wide < GM_WIDTH else (
                gv_sc, tile * wide - GM_WIDTH)
            dst[:, col:col + wide] = gf
        return run

    def gate_rows(r):
        def run():
            rows = slice(r * BLK, (r + 1) * BLK)
            gv = _row_rms(gv_sc[rows, :], v_gain_ref[:1, :])
            tiles = []
            for c in range(PAIRS):
                sl = slice(c * LANES, (c + 1) * LANES)
                vt = gv[:, sl]
                stacked = jnp.concatenate(
                    [jnp.where(low_half, vt, 0.0), jnp.where(low_half, 0.0, vt)],
                    axis=0).astype(bf16)
                s = jnp.dot(w_s_ref[c], stacked, preferred_element_type=f32) + b_s_ref[c]
                tiles.append(gu_sc[rows, sl] * s)
            gm = jnp.concatenate(tiles, axis=1)
            g_sc[rows, :] = _row_rms(gm, g_gain_ref[:1, :]).astype(bf16)
        return run

    early = [keys_values] + [gelu_tile(tile) for tile in range(2 * GM_WIDTH // wide)]
    gates = [gate_rows(r) for r in range(n_blk)]
    queries_ = [queries(tile) for tile in range(ATT_WIDTH // wide)]
    return early, gates, queries_


def _back_pieces(x_ref, sink_ref, ind_ref, a_gain_ref, w_o_ref, bias_sc, q_sc, kz_sc, vz_sc, g_sc,
                 att_sc, out_ref, seq_first, seq_last):
    f32 = jnp.float32
    bf16 = jnp.bfloat16
    n_blk = x_ref.shape[1] // BLK
    low_half = lax.broadcasted_iota(jnp.int32, (1, LANES), 1) < HEAD_DIM
    tiles_per_kv = PAIRS // N_KV

    def attend(qb, j):
        r0 = qb * BLK
        tiles = [j * tiles_per_kv + cl for cl in range(tiles_per_kv)]
        band = slice(r0, r0 + 3 * BLK)
        held = {}

        def scores():
            if qb == 0:
                variant = jnp.where(seq_first, 1, 0)
            elif qb == n_blk - 1:
                variant = jnp.where(seq_last, 2, 0)
            else:
                variant = 0
            kz = jnp.concatenate([kz_sc[2 * j, band, :], kz_sc[2 * j + 1, band, :]], axis=0)
            qs = jnp.concatenate(
                [q_sc[r0:r0 + BLK, c * LANES:(c + 1) * LANES] for c in tiles], axis=0)
            s2 = lax.dot_general(qs, kz, (((1,), (1,)), ((), ())),
                                 preferred_element_type=f32)
            e_rows = []
            sink_rows = []
            for cl, c in enumerate(tiles):
                es = []
                sinks = []
                for p in range(2):
                    h = 2 * c + p
                    s = (s2[cl * BLK:(cl + 1) * BLK, p * 3 * BLK:(p + 1) * 3 * BLK]
                         + bias_sc[variant, h])
                    m = jnp.max(s, axis=-1, keepdims=True)
                    es.append(jnp.exp2(s - m).astype(bf16))
                    sinks.append(jnp.exp2(sink_ref[h] * LOG2_E - m))
                e_rows.append(jnp.concatenate(es, axis=1))
                sink_rows.append(jnp.where(low_half, sinks[0], sinks[1]))
            held["e"] = jnp.concatenate(e_rows, axis=0)
            held["sink"] = sink_rows

        def values():
            vz = jnp.concatenate([vz_sc[2 * j, band, :], vz_sc[2 * j + 1, band, :]], axis=0)
            pv = jnp.dot(held["e"], jnp.concatenate([vz, ind_ref[...]], axis=1),
                         preferred_element_type=f32)
            for cl, c in enumerate(tiles):
                rows = slice(cl * BLK, (cl + 1) * BLK)
                att_sc[r0:r0 + BLK, c * LANES:(c + 1) * LANES] = (
                    pv[rows, :LANES] / (pv[rows, LANES:] + held["sink"][cl]))

        return scores, values

    normed = {}

    out_parts = 2
    out_width = D_MODEL // out_parts

    def project_out(half):
        def run():
            if "a" not in normed:
                normed["a"] = _row_rms(att_sc[...], a_gain_ref[:1, :]).astype(bf16)
            cols = slice(half * out_width, (half + 1) * out_width)
            y = jnp.dot(normed["a"], w_o_ref[:ATT_WIDTH, cols], preferred_element_type=f32)
            y = y + jnp.dot(g_sc[...], w_o_ref[ATT_WIDTH:, cols], preferred_element_type=f32)
            out_ref[0, :, cols] = x_ref[0, :, cols] + y
        return run

    attention = [attend(qb, j) for qb in range(n_blk) for j in range(N_KV)]
    outputs = [project_out(part) for part in range(out_parts)]
    return attention, outputs


def _mixer_kernel(table_ref, sink_ref, bucket_ref, ind_ref, xc_ref, xl_ref, norm1_ref,
                  w_in_ref, q_gain_ref, k_gain_ref, v_gain_ref, w_s_ref, b_s_ref, a_gain_ref,
                  g_gain_ref, w_o_ref, out_ref, bias_sc, att_sc, gu_sc, gv_sc, *stage_sc,
                  blocks_per_seq, n_steps):
    t = pl.program_id(0)
    last = n_steps - 1
    set_a, set_b = stage_sc[:4], stage_sc[4:]

    lag = jnp.maximum(t - 1, 0)
    pos = lax.rem(lag, blocks_per_seq)
    seq_first = pos == 0
    seq_last = pos == blocks_per_seq - 1

    def pieces(write_set, read_set):
        front = _front_pieces(
            xc_ref, norm1_ref, w_in_ref, q_gain_ref, k_gain_ref, v_gain_ref, w_s_ref, b_s_ref,
            g_gain_ref, gu_sc, gv_sc, read_set[1], read_set[2], *write_set)
        back = _back_pieces(xl_ref, sink_ref, ind_ref, a_gain_ref, w_o_ref, bias_sc,
                            *read_set, att_sc, out_ref, seq_first, seq_last)
        return front, back

    @pl.when(t == 0)
    def _first():
        _build_bias(table_ref, bucket_ref, bias_sc)
        for ref in set_b[1:3]:
            ref[...] = jnp.zeros(ref.shape, ref.dtype)
        (early, gates, query_tiles), _ = pieces(set_a, set_b)
        for piece in early + query_tiles + gates:
            piece()

    @pl.when(t == last)
    def _drain():
        read_set = set_b if last % 2 == 0 else set_a
        write_set = set_a if last % 2 == 0 else set_b
        _, (attention, outputs) = pieces(write_set, read_set)
        for scores, values in attention:
            scores()
            values()
        for project_out in outputs:
            project_out()

    def run(write_set, read_set):
        (early, gates, query_tiles), (attention, outputs) = pieces(write_set, read_set)
        n_items = len(attention)
        skip = lambda: None
        between = [early[0], early[1], skip, early[2], skip, early[3], skip, early[4]]
        for i in range(n_items + 1):
            if i < n_items:
                attention[i][0]()
                between[i]()
            if i >= 1:
                attention[i - 1][1]()
        w_first, w_last = outputs
        q_first, q_last = query_tiles
        for piece in (w_first, gates[0], q_first, gates[1], q_last, *gates[2:], w_last):
            piece()

    steady = jnp.logical_and(t > 0, t < last)
    parity = lax.rem(t, 2)

    @pl.when(jnp.logical_and(steady, parity == 0))
    def _even():
        run(set_a, set_b)

    @pl.when(jnp.logical_and(steady, parity == 1))
    def _odd():
        run(set_b, set_a)


def _ffn_kernel(x_ref, norm2_ref, w_gate_ref, w_up_ref, w_down_ref, out_ref):
    f32 = jnp.float32
    x = x_ref[...]
    h = _row_rms(x, norm2_ref[:1, :]).astype(jnp.bfloat16)
    gate = jnp.dot(h, w_gate_ref[...], preferred_element_type=f32)
    up = jnp.dot(h, w_up_ref[...], preferred_element_type=f32)
    act = (gate * (1.0 / (1.0 + jnp.exp(-gate))) * up).astype(jnp.bfloat16)
    out_ref[...] = x + jnp.dot(act, w_down_ref[...], preferred_element_type=f32)


def _resident(shape):
    zeros = (0,) * len(shape)
    return pl.BlockSpec(shape, lambda *_: zeros, pipeline_mode=pl.Buffered(1))


def _mixer(x, p):
    batch, seq, d = x.shape
    tb = MIX_TOKENS
    assert seq % tb == 0 and tb % (2 * BLK) == 0 and seq // BLK >= 2 and d == D_MODEL
    n_i = seq // tb
    n_t = batch * n_i

    def cur(t):
        tc = jnp.minimum(t, n_t - 1)
        return lax.div(tc, n_i), lax.rem(tc, n_i)

    def lagged(t):
        tl = jnp.maximum(t - 1, 0)
        return lax.div(tl, n_i), lax.rem(tl, n_i)

    def cur_map(t):
        b, i = cur(t)
        return b, i, 0

    def lag_map(t):
        b, i = lagged(t)
        return b, i, 0

    smem = pl.BlockSpec(memory_space=pltpu.SMEM)
    in_specs = [
        smem,
        smem,
        _resident((BLK, 3 * BLK)),
        _resident((6 * BLK, LANES)),
        pl.BlockSpec((1, tb, d), cur_map),
        pl.BlockSpec((1, tb, d), lag_map),
        _resident((SUBLANES, d)),
        _resident((d, IN_WIDTH)),
        _resident((SUBLANES, ATT_WIDTH)),
        _resident((SUBLANES, ATT_WIDTH)),
        _resident((SUBLANES, GM_WIDTH)),
        _resident((PAIRS, BLK, 2 * BLK)),
        _resident((PAIRS, BLK, LANES)),
        _resident((SUBLANES, ATT_WIDTH)),
        _resident((SUBLANES, GM_WIDTH)),
        _resident((d, d)),
    ]
    stage = [
        pltpu.VMEM((tb, ATT_WIDTH), jnp.bfloat16),
        pltpu.VMEM((2 * N_KV, tb + 2 * BLK, LANES), jnp.bfloat16),
        pltpu.VMEM((2 * N_KV, tb + 2 * BLK, LANES), jnp.bfloat16),
        pltpu.VMEM((tb, GM_WIDTH), jnp.bfloat16),
    ]
    scratch = [
        pltpu.VMEM((3, N_HEADS, BLK, 3 * BLK), jnp.float32),
        pltpu.VMEM((tb, ATT_WIDTH), jnp.float32),
        pltpu.VMEM((tb, GM_WIDTH), jnp.float32),
        pltpu.VMEM((tb, GM_WIDTH), jnp.float32),
    ] + stage + stage
    return pl.pallas_call(
        functools.partial(_mixer_kernel, blocks_per_seq=n_i, n_steps=n_t + 1),
        grid=(n_t + 1,),
        in_specs=in_specs,
        out_specs=pl.BlockSpec((1, tb, d), lag_map),
        out_shape=jax.ShapeDtypeStruct(x.shape, x.dtype),
        scratch_shapes=scratch,
        compiler_params=pltpu.CompilerParams(
            dimension_semantics=("arbitrary",),
            vmem_limit_bytes=VMEM_LIMIT_BYTES),
        name="mixer",
    )(p["table"], p["sink"], p["bucket"], p["indicator"], x, x, p["norm1"], p["w_in"], p["q_gain"],
      p["k_gain"], p["v_gain"], p["w_s"], p["b_s"], p["a_gain"], p["g_gain"], p["w_o"])


def _ffn(x2d, p):
    n, d = x2d.shape
    tm = FFN_TOKENS
    assert n % tm == 0
    d_ff = p["w_gate"].shape[1]
    return pl.pallas_call(
        _ffn_kernel,
        grid=(n // tm,),
        in_specs=[
            pl.BlockSpec((tm, d), lambda i: (i, 0)),
            _resident((SUBLANES, d)),
            _resident((d, d_ff)),
            _resident((d, d_ff)),
            _resident((d_ff, d)),
        ],
        out_specs=pl.BlockSpec((tm, d), lambda i: (i, 0)),
        out_shape=jax.ShapeDtypeStruct(x2d.shape, x2d.dtype),
        compiler_params=pltpu.CompilerParams(
            dimension_semantics=("arbitrary",),
            vmem_limit_bytes=VMEM_LIMIT_BYTES),
        name="ffn",
    )(x2d, p["norm2"], p["w_gate"], p["w_up"], p["w_down"])


def _layer_params(rel_bias_table, norm1, w_in, q_gain, k_gain, sink, v_gain, w_s, b_s,
                  attn_out_gain, gmlp_out_gain, w_o, norm2, w_gate, w_up, w_down):
    bf16 = jnp.bfloat16

    def row(v, width=None):
        v = v.reshape(1, -1).astype(jnp.float32)
        if width is not None:
            v = jnp.pad(v, ((0, 0), (0, width - v.shape[1])))
        return jnp.broadcast_to(v, (SUBLANES, v.shape[1]))

    w_pairs = jnp.concatenate([w_s[0::2], w_s[1::2]], axis=-1).astype(bf16)
    b_pairs = jnp.repeat(jnp.transpose(b_s.reshape(PAIRS, 2, BLK), (0, 2, 1)), HEAD_DIM, axis=-1)
    return {
        "table": rel_bias_table.astype(jnp.float32),
        "sink": sink.astype(jnp.float32),
        "bucket": jnp.asarray(_t5_bucket_table()),
        "indicator": jnp.asarray(_key_set_indicator(), dtype=bf16),
        "norm1": row(norm1),
        "w_in": w_in.astype(bf16),
        "q_gain": row(jnp.tile(q_gain, N_HEADS)),
        "k_gain": row(jnp.tile(k_gain, N_KV), ATT_WIDTH),
        "v_gain": row(v_gain),
        "w_s": w_pairs,
        "b_s": b_pairs.astype(jnp.float32),
        "a_gain": row(attn_out_gain),
        "g_gain": row(gmlp_out_gain),
        "w_o": w_o.astype(bf16),
        "norm2": row(norm2),
        "w_gate": w_gate.astype(bf16),
        "w_up": w_up.astype(bf16),
        "w_down": w_down.astype(bf16),
    }


def kernel(x_prompt, x_sample, rel_bias_table, norm1, w_in, q_gain, k_gain, sink, v_gain, w_s, b_s,
           attn_out_gain, gmlp_out_gain, w_o, norm2, w_gate, w_up, w_down):
    layers = [
        _layer_params(rel_bias_table, norm1[l], w_in[l], q_gain[l], k_gain[l], sink[l], v_gain[l],
                      w_s[l], b_s[l], attn_out_gain[l], gmlp_out_gain[l], w_o[l], norm2[l],
                      w_gate[l], w_up[l], w_down[l])
        for l in range(norm1.shape[0])]
    outs = []
    for x in (x_prompt, x_sample):
        for p in layers:
            x = _mixer(x, p)
            x = _ffn(x.reshape(-1, D_MODEL), p).reshape(x.shape)
        outs.append(x)
    return tuple(outs)
```

```python
import functools
import math

import jax
import jax.numpy as jnp
import numpy as np
from jax import lax
from jax.experimental import pallas as pl
from jax.experimental.pallas import tpu as pltpu

D_MODEL = 1024
HEAD_DIM = 64
N_HEADS = 8
N_KV = 2
ATT_WIDTH = N_HEADS * HEAD_DIM
KV_WIDTH = N_KV * HEAD_DIM
GM_HEADS = 8
GM_WIDTH = GM_HEADS * HEAD_DIM
IN_WIDTH = ATT_WIDTH + 2 * KV_WIDTH + 2 * GM_WIDTH
BLK = 128
N_BUCKETS = 32
MAX_DIST = 128
EPS = 1e-6
NEG = -1e30
LOG2_E = math.log2(math.e)

LANES = 128
SUBLANES = 8
PAIRS = ATT_WIDTH // LANES
MIX_TOKENS = 512
FFN_TOKENS = 512
VMEM_LIMIT_BYTES = 56 * 1024 * 1024

_Q_OFF = 0
_KV_OFF = ATT_WIDTH
_G_OFF = ATT_WIDTH + 2 * KV_WIDTH


def _t5_bucket_table():
    rel = (np.arange(3 * BLK) - BLK)[None, :] - np.arange(BLK)[:, None]
    half = N_BUCKETS // 2
    max_exact = half // 2
    ret = (rel > 0).astype(np.int32) * half
    n = np.abs(rel)
    large = max_exact + (np.log(np.maximum(n, 1).astype(np.float32) / max_exact)
                         / np.log(MAX_DIST / max_exact) * (half - max_exact)).astype(np.int32)
    large = np.minimum(large, half - 1)
    bucket = (ret + np.where(n < max_exact, n, large)).astype(np.int32)
    return np.where(np.abs(rel) <= BLK, bucket, -1).astype(np.int32)


def _key_set_indicator():
    key_is_low = np.arange(6 * BLK)[:, None] < 3 * BLK
    lane_is_low = np.arange(LANES)[None, :] < HEAD_DIM
    return (key_is_low == lane_is_low).astype(np.float32)


def _row_rms(x, gain):
    ms = jnp.mean(x * x, axis=-1, keepdims=True)
    return x * lax.rsqrt(ms + EPS) * gain


def _head_rms(x, gain, low_half):
    x2 = x * x
    s_lo = jnp.sum(jnp.where(low_half, x2, 0.0), axis=-1, keepdims=True)
    s_hi = jnp.sum(jnp.where(low_half, 0.0, x2), axis=-1, keepdims=True)
    r_lo = lax.rsqrt(s_lo * (1.0 / HEAD_DIM) + EPS)
    r_hi = lax.rsqrt(s_hi * (1.0 / HEAD_DIM) + EPS)
    return x * jnp.where(low_half, r_lo, r_hi) * gain


def _gelu_tanh(x):
    a = -2.0 * math.sqrt(2.0 / math.pi) * LOG2_E
    w = x * (x * x * (a * 0.044715) + a)
    return x * (1.0 / (1.0 + jnp.exp2(w)))


def _build_bias(table_ref, bucket_ref, bias_sc):
    f32 = jnp.float32
    neg = jnp.full((BLK, BLK), NEG, f32)
    for h in range(N_HEADS):
        for part in range(3):
            bk = bucket_ref[:, part * BLK:(part + 1) * BLK]

            def body(b, acc):
                return jnp.where(bk == b, table_ref[b, h] * LOG2_E, acc)

            acc = lax.fori_loop(0, N_BUCKETS, body, neg)
            sl = slice(part * BLK, (part + 1) * BLK)
            bias_sc[0, h, :, sl] = acc
            bias_sc[1, h, :, sl] = neg if part == 0 else acc
            bias_sc[2, h, :, sl] = neg if part == 2 else acc


def _front_pieces(xc_ref, norm1_ref, w_in_ref, q_gain_ref, k_gain_ref, v_gain_ref, w_s_ref,
                  b_s_ref, g_gain_ref, gu_sc, gv_sc, kz_prev, vz_prev, q_sc, kz_sc, vz_sc, g_sc):
    f32 = jnp.float32
    bf16 = jnp.bfloat16
    tb = xc_ref.shape[1]
    n_blk = tb // BLK
    low_half = lax.broadcasted_iota(jnp.int32, (1, LANES), 1) < HEAD_DIM
    wide = 2 * LANES
    state = {}

    def project(h, off):
        return jnp.dot(h, w_in_ref[:, off:off + wide], preferred_element_type=f32)

    def keys_values():
        hc = _row_rms(xc_ref[0], norm1_ref[:1, :]).astype(bf16)
        state["hc"] = hc
        kv = project(hc, _KV_OFF)
        kn = _head_rms(kv[:, :KV_WIDTH], k_gain_ref[:1, :KV_WIDTH], low_half)
        vv = kv[:, KV_WIDTH:]
        kn_sw = pltpu.roll(kn, HEAD_DIM, axis=1)
        vv_sw = pltpu.roll(vv, HEAD_DIM, axis=1)
        zero = jnp.zeros_like(kn)
        tail = slice(tb, tb + BLK)
        for idx, (src_k, src_v, keep_low) in enumerate(
                ((kn, vv, True), (kn_sw, vv_sw, False), (kn_sw, vv_sw, True), (kn, vv, False))):
            keep = low_half if keep_low else jnp.logical_not(low_half)
            for src, dst, dst_prev in ((src_k, kz_sc, kz_prev), (src_v, vz_sc, vz_prev)):
                laid = jnp.where(keep, src, zero).astype(bf16)
                dst[idx, BLK:BLK + tb, :] = laid
                dst_prev[idx, BLK + tb:, :] = laid[:BLK]
                dst[idx, :BLK, :] = dst_prev[idx, tail, :]

    def queries(tile):
        def run():
            qf = project(state["hc"], _Q_OFF + tile * wide)
            for cl in range(wide // LANES):
                sl = slice(tile * wide + cl * LANES, tile * wide + (cl + 1) * LANES)
                gain = q_gain_ref[:1, sl] * (LOG2_E / math.sqrt(HEAD_DIM))
                q_sc[:, sl] = _head_rms(qf[:, cl * LANES:(cl + 1) * LANES], gain,
                                        low_half).astype(bf16)
        return run

    def gelu_tile(tile):
        def run():
            gf = _gelu_tanh(project(state["hc"], _G_OFF + tile * wide))
            dst, col = (gu_sc, tile * wide) if tile * wide < GM_WIDTH else (
                gv_sc, tile * wide - GM_WIDTH)
            dst[:, col:col + wide] = gf
        return run

    def gate_tile(c):
        def run():
            sl = slice(c * LANES, (c + 1) * LANES)
            if c == 0:
                gv_sc[...] = _row_rms(gv_sc[...], v_gain_ref[:1, :])
            stacked = []
            for r in range(n_blk):
                vt = gv_sc[r * BLK:(r + 1) * BLK, sl]
                stacked.append(jnp.concatenate(
                    [jnp.where(low_half, vt, 0.0), jnp.where(low_half, 0.0, vt)],
                    axis=0).astype(bf16))
            s = jnp.dot(w_s_ref[c], jnp.concatenate(stacked, axis=1),
                        preferred_element_type=f32)
            for r in range(n_blk):
                rows = slice(r * BLK, (r + 1) * BLK)
                gu_sc[rows, sl] = gu_sc[rows, sl] * (s[:, r * LANES:(r + 1) * LANES] + b_s_ref[c])
            if c == PAIRS - 1:
                g_sc[...] = _row_rms(gu_sc[...], g_gain_ref[:1, :]).astype(bf16)
        return run

    early =[keys_values] + [gelu_tile(tile) for tile in range(2 * GM_WIDTH // wide)]
    gates = [gate_tile(c) for c in range(PAIRS)]
    queries_ = [queries(tile) for tile in range(ATT_WIDTH // wide)]
    return early, gates, queries_


def _back_pieces(x_ref, sink_ref, ind_ref, a_gain_ref, w_o_ref, bias_sc, q_sc, kz_sc, vz_sc, g_sc,
                 att_sc, out_ref, seq_first, seq_last):
    f32 = jnp.float32
    bf16 = jnp.bfloat16
    n_blk = x_ref.shape[1] // BLK
    low_half = lax.broadcasted_iota(jnp.int32, (1, LANES), 1) < HEAD_DIM
    tiles_per_kv = PAIRS // N_KV

    def attend(qb, j):
        r0 = qb * BLK
        tiles = [j * tiles_per_kv + cl for cl in range(tiles_per_kv)]
        band = slice(r0, r0 + 3 * BLK)
        held = {}

        def scores():
            if qb == 0:
                variant = jnp.where(seq_first, 1, 0)
            elif qb == n_blk - 1:
                variant = jnp.where(seq_last, 2, 0)
            else:
                variant = 0
            kz = jnp.concatenate([kz_sc[2 * j, band, :], kz_sc[2 * j + 1, band, :]], axis=0)
            qs = jnp.concatenate(
                [q_sc[r0:r0 + BLK, c * LANES:(c + 1) * LANES] for c in tiles], axis=0)
            s2 = lax.dot_general(qs, kz, (((1,), (1,)), ((), ())),
                                 preferred_element_type=f32)
            e_rows = []
            sink_rows = []
            for cl, c in enumerate(tiles):
                es = []
                sinks = []
                for p in range(2):
                    h = 2 * c + p
                    s = (s2[cl * BLK:(cl + 1) * BLK, p * 3 * BLK:(p + 1) * 3 * BLK]
                         + bias_sc[variant, h])
                    m = jnp.max(s, axis=-1, keepdims=True)
                    es.append(jnp.exp2(s - m).astype(bf16))
                    sinks.append(jnp.exp2(sink_ref[h] * LOG2_E - m))
                e_rows.append(jnp.concatenate(es, axis=1))
                sink_rows.append(jnp.where(low_half, sinks[0], sinks[1]))
            held["e"] = jnp.concatenate(e_rows, axis=0)
            held["sink"] = sink_rows

        def values():
            vz = jnp.concatenate([vz_sc[2 * j, band, :], vz_sc[2 * j + 1, band, :]], axis=0)
            pv = jnp.dot(held["e"], jnp.concatenate([vz, ind_ref[...]], axis=1),
                         preferred_element_type=f32)
            for cl, c in enumerate(tiles):
                rows = slice(cl * BLK, (cl + 1) * BLK)
                att_sc[r0:r0 + BLK, c * LANES:(c + 1) * LANES] = (
                    pv[rows, :LANES] / (pv[rows, LANES:] + held["sink"][cl]))

        return scores, values

    normed = {}
    out_parts = 2
    out_width = D_MODEL // out_parts

    def project_out(part):
        def run():
            if "a" not in normed:
                normed["a"] = _row_rms(att_sc[...], a_gain_ref[:1, :]).astype(bf16)
            cols = slice(part * out_width, (part + 1) * out_width)
            y = jnp.dot(normed["a"], w_o_ref[:ATT_WIDTH, cols], preferred_element_type=f32)
            y = y + jnp.dot(g_sc[...], w_o_ref[ATT_WIDTH:, cols], preferred_element_type=f32)
            out_ref[0, :, cols] = x_ref[0, :, cols] + y
        return run

    attention = [attend(qb, j) for qb in range(n_blk) for j in range(N_KV)]
    outputs = [project_out(part) for part in range(out_parts)]
    return attention, outputs


def _mixer_kernel(table_ref, sink_ref, bucket_ref, ind_ref, xc_ref, xl_ref, norm1_ref,
                  w_in_ref, q_gain_ref, k_gain_ref, v_gain_ref, w_s_ref, b_s_ref, a_gain_ref,
                  g_gain_ref, w_o_ref, out_ref, bias_sc, att_sc, gu_sc, gv_sc, *stage_sc,
                  blocks_per_seq, n_steps):
    t = pl.program_id(0)
    last = n_steps - 1
    set_a, set_b = stage_sc[:4], stage_sc[4:]

    lag = jnp.maximum(t - 1, 0)
    pos = lax.rem(lag, blocks_per_seq)
    seq_first = pos == 0
    seq_last = pos == blocks_per_seq - 1

    def pieces(write_set, read_set):
        front = _front_pieces(
            xc_ref, norm1_ref, w_in_ref, q_gain_ref, k_gain_ref, v_gain_ref, w_s_ref, b_s_ref,
            g_gain_ref, gu_sc, gv_sc, read_set[1], read_set[2], *write_set)
        back = _back_pieces(xl_ref, sink_ref, ind_ref, a_gain_ref, w_o_ref, bias_sc,
                            *read_set, att_sc, out_ref, seq_first, seq_last)
        return front, back

    @pl.when(t == 0)
    def _first():
        _build_bias(table_ref, bucket_ref, bias_sc)
        for ref in set_b[1:3]:
            ref[...] = jnp.zeros(ref.shape, ref.dtype)
        (early, gates, query_tiles), _ = pieces(set_a, set_b)
        for piece in early + query_tiles + gates:
            piece()

    @pl.when(t == last)
    def _drain():
        read_set = set_b if last % 2 == 0 else set_a
        write_set = set_a if last % 2 == 0 else set_b
        _, (attention, outputs) = pieces(write_set, read_set)
        for scores, values in attention:
            scores()
            values()
        for project_out in outputs:
            project_out()

    def run(write_set, read_set):
        (early, gates, query_tiles), (attention, outputs) = pieces(write_set, read_set)
        n_items = len(attention)
        between = [lambda: None] * n_items
        between[0] = early[0]
        stride = n_items // len(early[1:])
        for k, piece in enumerate(early[1:]):
            between[1 + k * stride] = piece
        for i in range(n_items + 1):
            if i < n_items:
                attention[i][0]()
                between[i]()
            if i >= 1:
                attention[i - 1][1]()
        w_first, w_last = outputs
        q_first, q_last = query_tiles
        for piece in (w_first, gates[0], q_first, gates[1], q_last, *gates[2:], w_last):
            piece()

    steady = jnp.logical_and(t > 0, t < last)
    parity = lax.rem(t, 2)

    @pl.when(jnp.logical_and(steady, parity == 0))
    def _even():
        run(set_a, set_b)

    @pl.when(jnp.logical_and(steady, parity == 1))
    def _odd():
        run(set_b, set_a)


def _ffn_kernel(x_ref, norm2_ref, w_gate_ref, w_up_ref, w_down_ref, out_ref):
    f32 = jnp.float32
    x = x_ref[...]
    h = _row_rms(x, norm2_ref[:1, :]).astype(jnp.bfloat16)
    gate = jnp.dot(h, w_gate_ref[...], preferred_element_type=f32)
    up = jnp.dot(h, w_up_ref[...], preferred_element_type=f32)
    act = (gate * (1.0 / (1.0 + jnp.exp(-gate))) * up).astype(jnp.bfloat16)
    out_ref[...] = x + jnp.dot(act, w_down_ref[...], preferred_element_type=f32)


def _resident(shape):
    zeros = (0,) * len(shape)
    return pl.BlockSpec(shape, lambda *_: zeros, pipeline_mode=pl.Buffered(1))


def _mixer(x, p):
    batch, seq, d = x.shape
    tb = MIX_TOKENS
    assert seq % tb == 0 and tb % (2 * BLK) == 0 and seq // BLK >= 2 and d == D_MODEL
    n_i = seq // tb
    n_t = batch * n_i
    assert n_t >= 3

    def cur(t):
        tc = jnp.minimum(t, n_t - 1)
        return lax.div(tc, n_i), lax.rem(tc, n_i)

    def lagged(t):
        tl = jnp.maximum(t - 1, 0)
        return lax.div(tl, n_i), lax.rem(tl, n_i)

    def cur_map(t):
        b, i = cur(t)
        return b, i, 0

    def lag_map(t):
        b, i = lagged(t)
        return b, i, 0

    smem = pl.BlockSpec(memory_space=pltpu.SMEM)
    in_specs = [
        smem,
        smem,
        _resident((BLK, 3 * BLK)),
        _resident((6 * BLK, LANES)),
        pl.BlockSpec((1, tb, d), cur_map),
        pl.BlockSpec((1, tb, d), lag_map),
        _resident((SUBLANES, d)),
        _resident((d, IN_WIDTH)),
        _resident((SUBLANES, ATT_WIDTH)),
        _resident((SUBLANES, ATT_WIDTH)),
        _resident((SUBLANES, GM_WIDTH)),
        _resident((PAIRS, BLK, 2 * BLK)),
        _resident((PAIRS, BLK, LANES)),
        _resident((SUBLANES, ATT_WIDTH)),
        _resident((SUBLANES, GM_WIDTH)),
        _resident((d, d)),
    ]
    stage = [
        pltpu.VMEM((tb, ATT_WIDTH), jnp.bfloat16),
        pltpu.VMEM((2 * N_KV, tb + 2 * BLK, LANES), jnp.bfloat16),
        pltpu.VMEM((2 * N_KV, tb + 2 * BLK, LANES), jnp.bfloat16),
        pltpu.VMEM((tb, GM_WIDTH), jnp.bfloat16),
    ]
    scratch = [
        pltpu.VMEM((3, N_HEADS, BLK, 3 * BLK), jnp.float32),
        pltpu.VMEM((tb, ATT_WIDTH), jnp.float32),
        pltpu.VMEM((tb, GM_WIDTH), jnp.float32),
        pltpu.VMEM((tb, GM_WIDTH), jnp.float32),
    ] + stage + stage
    return pl.pallas_call(
        functools.partial(_mixer_kernel, blocks_per_seq=n_i, n_steps=n_t + 1),
        grid=(n_t + 1,),
        in_specs=in_specs,
        out_specs=pl.BlockSpec((1, tb, d), lag_map),
        out_shape=jax.ShapeDtypeStruct(x.shape, x.dtype),
        scratch_shapes=scratch,
        compiler_params=pltpu.CompilerParams(
            dimension_semantics=("arbitrary",),
            vmem_limit_bytes=VMEM_LIMIT_BYTES),
        name="mixer",
    )(p["table"], p["sink"], p["bucket"], p["indicator"], x, x, p["norm1"], p["w_in"], p["q_gain"],
      p["k_gain"], p["v_gain"], p["w_s"], p["b_s"], p["a_gain"], p["g_gain"], p["w_o"])


def _ffn(x2d, p):
    n, d = x2d.shape
    tm = FFN_TOKENS
    assert n % tm == 0
    d_ff = p["w_gate"].shape[1]
    return pl.pallas_call(
        _ffn_kernel,
        grid=(n // tm,),
        in_specs=[
            pl.BlockSpec((tm, d), lambda i: (i, 0)),
            _resident((SUBLANES, d)),
            _resident((d, d_ff)),
            _resident((d, d_ff)),
            _resident((d_ff, d)),
        ],
        out_specs=pl.BlockSpec((tm, d), lambda i: (i, 0)),
        out_shape=jax.ShapeDtypeStruct(x2d.shape, x2d.dtype),
        compiler_params=pltpu.CompilerParams(
            dimension_semantics=("arbitrary",),
            vmem_limit_bytes=VMEM_LIMIT_BYTES),
        name="ffn",
    )(x2d, p["norm2"], p["w_gate"], p["w_up"], p["w_down"])


def _layer_params(rel_bias_table, norm1, w_in, q_gain, k_gain, sink, v_gain, w_s, b_s,
                  attn_out_gain, gmlp_out_gain, w_o, norm2, w_gate, w_up, w_down):
    bf16 = jnp.bfloat16

    def row(v, width=None):
        v = v.reshape(1, -1).astype(jnp.float32)
        if width is not None:
            v = jnp.pad(v, ((0, 0), (0, width - v.shape[1])))
        return jnp.broadcast_to(v, (SUBLANES, v.shape[1]))

    w_pairs = jnp.concatenate([w_s[0::2], w_s[1::2]], axis=-1).astype(bf16)
    b_pairs = jnp.repeat(jnp.transpose(b_s.reshape(PAIRS, 2, BLK), (0, 2, 1)), HEAD_DIM, axis=-1)
    return {
        "table": rel_bias_table.astype(jnp.float32),
        "sink": sink.astype(jnp.float32),
        "bucket": jnp.asarray(_t5_bucket_table()),
        "indicator": jnp.asarray(_key_set_indicator(), dtype=bf16),
        "norm1": row(norm1),
        "w_in": w_in.astype(bf16),
        "q_gain": row(jnp.tile(q_gain, N_HEADS)),
        "k_gain": row(jnp.tile(k_gain, N_KV), ATT_WIDTH),
        "v_gain": row(v_gain),
        "w_s": w_pairs,
        "b_s": b_pairs.astype(jnp.float32),
        "a_gain": row(attn_out_gain),
        "g_gain": row(gmlp_out_gain),
        "w_o": w_o.astype(bf16),
        "norm2": row(norm2),
        "w_gate": w_gate.astype(bf16),
        "w_up": w_up.astype(bf16),
        "w_down": w_down.astype(bf16),
    }


def kernel(x_prompt, x_sample, rel_bias_table, norm1, w_in, q_gain, k_gain, sink, v_gain, w_s, b_s,
           attn_out_gain, gmlp_out_gain, w_o, norm2, w_gate, w_up, w_down):
    layers = [
        _layer_params(rel_bias_table, norm1[l], w_in[l], q_gain[l], k_gain[l], sink[l], v_gain[l],
                      w_s[l], b_s[l], attn_out_gain[l], gmlp_out_gain[l], w_o[l], norm2[l],
                      w_gate[l], w_up[l], w_down[l])
        for l in range(norm1.shape[0])]
    outs = []
    for x in (x_prompt, x_sample):
        for p in layers:
            x = _mixer(x, p)
            x = _ffn(x.reshape(-1, D_MODEL), p).reshape(x.shape)
        outs.append(x)
    return tuple(outs)
```

```python
import functools
import math

import jax
import jax.numpy as jnp
import numpy as np
from jax import lax
from jax.experimental import pallas as pl
from jax.experimental.pallas import tpu as pltpu

D_MODEL = 1024
HEAD_DIM = 64
N_HEADS = 8
N_KV = 2
ATT_WIDTH = N_HEADS * HEAD_DIM
KV_WIDTH = N_KV * HEAD_DIM
GM_HEADS = 8
GM_WIDTH = GM_HEADS * HEAD_DIM
IN_WIDTH = ATT_WIDTH + 2 * KV_WIDTH + 2 * GM_WIDTH
BLK = 128
N_BUCKETS = 32
MAX_DIST = 128
EPS = 1e-6
NEG = -1e30
LOG2_E = math.log2(math.e)

LANES = 128
SUBLANES = 8
PAIRS = ATT_WIDTH // LANES
MIX_TOKENS = 512
FFN_TOKENS = 512
FFN_CHUNKS = 4
VMEM_LIMIT_BYTES = 56 * 1024 * 1024

_Q_OFF = 0
_KV_OFF = ATT_WIDTH
_G_OFF = ATT_WIDTH + 2 * KV_WIDTH


def _t5_bucket_table():
    rel = (np.arange(3 * BLK) - BLK)[None, :] - np.arange(BLK)[:, None]
    half = N_BUCKETS // 2
    max_exact = half // 2
    ret = (rel > 0).astype(np.int32) * half
    n = np.abs(rel)
    large = max_exact + (np.log(np.maximum(n, 1).astype(np.float32) / max_exact)
                         / np.log(MAX_DIST / max_exact) * (half - max_exact)).astype(np.int32)
    large = np.minimum(large, half - 1)
    bucket = (ret + np.where(n < max_exact, n, large)).astype(np.int32)
    return np.where(np.abs(rel) <= BLK, bucket, -1).astype(np.int32)


def _key_set_indicator():
    key_is_low = np.arange(6 * BLK)[:, None] < 3 * BLK
    lane_is_low = np.arange(LANES)[None, :] < HEAD_DIM
    return (key_is_low == lane_is_low).astype(np.float32)


def _row_rms(x, gain):
    ms = jnp.mean(x * x, axis=-1, keepdims=True)
    return x * lax.rsqrt(ms + EPS) * gain


def _head_rms(x, gain, low_half):
    x2 = x * x
    s_lo = jnp.sum(jnp.where(low_half, x2, 0.0), axis=-1, keepdims=True)
    s_hi = jnp.sum(jnp.where(low_half, 0.0, x2), axis=-1, keepdims=True)
    r_lo = lax.rsqrt(s_lo * (1.0 / HEAD_DIM) + EPS)
    r_hi = lax.rsqrt(s_hi * (1.0 / HEAD_DIM) + EPS)
    return x * jnp.where(low_half, r_lo, r_hi) * gain


def _gelu_tanh(x):
    a = -2.0 * math.sqrt(2.0 / math.pi) * LOG2_E
    w = x * (x * x * (a * 0.044715) + a)
    return x * (1.0 / (1.0 + jnp.exp2(w)))


def _build_bias(table_ref, bucket_ref, bias_sc):
    f32 = jnp.float32
    neg = jnp.full((BLK, BLK), NEG, f32)
    for h in range(N_HEADS):
        for part in range(3):
            bk = bucket_ref[:, part * BLK:(part + 1) * BLK]

            def body(b, acc):
                return jnp.where(bk == b, table_ref[b, h] * LOG2_E, acc)

            acc = lax.fori_loop(0, N_BUCKETS, body, neg)
            sl = slice(part * BLK, (part + 1) * BLK)
            bias_sc[0, h, :, sl] = acc
            bias_sc[1, h, :, sl] = neg if part == 0 else acc
            bias_sc[2, h, :, sl] = neg if part == 2 else acc


def _front_pieces(xc_ref, norm1_ref, w_in_ref, q_gain_ref, k_gain_ref, v_gain_ref, w_s_ref,
                  b_s_ref, g_gain_ref, gu_sc, gv_sc, kz_prev, vz_prev, q_sc, kz_sc, vz_sc, g_sc):
    f32 = jnp.float32
    bf16 = jnp.bfloat16
    tb = xc_ref.shape[1]
    n_blk = tb // BLK
    low_half = lax.broadcasted_iota(jnp.int32, (1, LANES), 1) < HEAD_DIM
    wide = 2 * LANES
    state = {}

    def project(h, off):
        return jnp.dot(h, w_in_ref[:, off:off + wide], preferred_element_type=f32)

    def keys_values():
        hc = _row_rms(xc_ref[0], norm1_ref[:1, :]).astype(bf16)
        state["hc"] = hc
        kv = project(hc, _KV_OFF)
        kn = _head_rms(kv[:, :KV_WIDTH], k_gain_ref[:1, :KV_WIDTH], low_half)
        vv = kv[:, KV_WIDTH:]
        kn_sw = pltpu.roll(kn, HEAD_DIM, axis=1)
        vv_sw = pltpu.roll(vv, HEAD_DIM, axis=1)
        zero = jnp.zeros_like(kn)
        tail = slice(tb, tb + BLK)
        for idx, (src_k, src_v, keep_low) in enumerate(
                ((kn, vv, True), (kn_sw, vv_sw, False), (kn_sw, vv_sw, True), (kn, vv, False))):
            keep = low_half if keep_low else jnp.logical_not(low_half)
            for src, dst, dst_prev in ((src_k, kz_sc, kz_prev), (src_v, vz_sc, vz_prev)):
                laid = jnp.where(keep, src, zero).astype(bf16)
                dst[idx, BLK:BLK + tb, :] = laid
                dst_prev[idx, BLK + tb:, :] = laid[:BLK]
                dst[idx, :BLK, :] = dst_prev[idx, tail, :]

    def queries(tile):
        def run():
            qf = project(state["hc"], _Q_OFF + tile * wide)
            for cl in range(wide // LANES):
                sl = slice(tile * wide + cl * LANES, tile * wide + (cl + 1) * LANES)
                gain = q_gain_ref[:1, sl] * (LOG2_E / math.sqrt(HEAD_DIM))
                q_sc[:, sl] = _head_rms(qf[:, cl * LANES:(cl + 1) * LANES], gain,
                                        low_half).astype(bf16)
        return run

    def gelu_tile(tile):
        def run():
            gf = _gelu_tanh(project(state["hc"], _G_OFF + tile * wide))
            dst, col = (gu_sc, tile * wide) if tile * wide < GM_WIDTH else (
                gv_sc, tile * wide - GM_WIDTH)
            dst[:, col:col + wide] = gf
        return run

    def gate_tile(c):
        def run():
            sl = slice(c * LANES, (c + 1) * LANES)
            if c == 0:
                gv_sc[...] = _row_rms(gv_sc[...], v_gain_ref[:1, :])
            stacked = []
            for r in range(n_blk):
                vt = gv_sc[r * BLK:(r + 1) * BLK, sl]
                stacked.append(jnp.concatenate(
                    [jnp.where(low_half, vt, 0.0), jnp.where(low_half, 0.0, vt)],
                    axis=0).astype(bf16))
            s = jnp.dot(w_s_ref[c], jnp.concatenate(stacked, axis=1),
                        preferred_element_type=f32)
            for r in range(n_blk):
                rows = slice(r * BLK, (r + 1) * BLK)
                gu_sc[rows, sl] = gu_sc[rows, sl] * (s[:, r * LANES:(r + 1) * LANES] + b_s_ref[c])
            if c == PAIRS - 1:
                g_sc[...] = _row_rms(gu_sc[...], g_gain_ref[:1, :]).astype(bf16)
        return run

    early =[keys_values] + [gelu_tile(tile) for tile in range(2 * GM_WIDTH // wide)]
    gates = [gate_tile(c) for c in range(PAIRS)]
    queries_ = [queries(tile) for tile in range(ATT_WIDTH // wide)]
    return early, gates, queries_


def _back_pieces(x_ref, sink_ref, ind_ref, a_gain_ref, w_o_ref, bias_sc, q_sc, kz_sc, vz_sc, g_sc,
                 att_sc, out_ref, seq_first, seq_last):
    f32 = jnp.float32
    bf16 = jnp.bfloat16
    n_blk = x_ref.shape[1] // BLK
    low_half = lax.broadcasted_iota(jnp.int32, (1, LANES), 1) < HEAD_DIM
    tiles_per_kv = PAIRS // N_KV

    def attend(qb, j):
        r0 = qb * BLK
        tiles = [j * tiles_per_kv + cl for cl in range(tiles_per_kv)]
        band = slice(r0, r0 + 3 * BLK)
        held = {}

        def scores():
            if qb == 0:
                variant = jnp.where(seq_first, 1, 0)
            elif qb == n_blk - 1:
                variant = jnp.where(seq_last, 2, 0)
            else:
                variant = 0
            kz = jnp.concatenate([kz_sc[2 * j, band, :], kz_sc[2 * j + 1, band, :]], axis=0)
            qs = jnp.concatenate(
                [q_sc[r0:r0 + BLK, c * LANES:(c + 1) * LANES] for c in tiles], axis=0)
            s2 = lax.dot_general(qs, kz, (((1,), (1,)), ((), ())),
                                 preferred_element_type=f32)
            e_rows = []
            sink_rows = []
            for cl, c in enumerate(tiles):
                es = []
                sinks = []
                for p in range(2):
                    h = 2 * c + p
                    s = (s2[cl * BLK:(cl + 1) * BLK, p * 3 * BLK:(p + 1) * 3 * BLK]
                         + bias_sc[variant, h])
                    m = jnp.max(s, axis=-1, keepdims=True)
                    es.append(jnp.exp2(s - m).astype(bf16))
                    sinks.append(jnp.exp2(sink_ref[h] * LOG2_E - m))
                e_rows.append(jnp.concatenate(es, axis=1))
                sink_rows.append(jnp.where(low_half, sinks[0], sinks[1]))
            held["e"] = jnp.concatenate(e_rows, axis=0)
            held["sink"] = sink_rows

        def values():
            vz = jnp.concatenate([vz_sc[2 * j, band, :], vz_sc[2 * j + 1, band, :]], axis=0)
            pv = jnp.dot(held["e"], jnp.concatenate([vz, ind_ref[...]], axis=1),
                         preferred_element_type=f32)
            for cl, c in enumerate(tiles):
                rows = slice(cl * BLK, (cl + 1) * BLK)
                att_sc[r0:r0 + BLK, c * LANES:(c + 1) * LANES] = (
                    pv[rows, :LANES] / (pv[rows, LANES:] + held["sink"][cl]))

        return scores, values

    normed = {}
    out_parts = 2
    out_width = D_MODEL // out_parts

    def project_out(part):
        def run():
            if "a" not in normed:
                normed["a"] = _row_rms(att_sc[...], a_gain_ref[:1, :]).astype(bf16)
            cols = slice(part * out_width, (part + 1) * out_width)
            y = jnp.dot(normed["a"], w_o_ref[:ATT_WIDTH, cols], preferred_element_type=f32)
            y = y + jnp.dot(g_sc[...], w_o_ref[ATT_WIDTH:, cols], preferred_element_type=f32)
            out_ref[0, :, cols] = x_ref[0, :, cols] + y
        return run

    attention = [attend(qb, j) for qb in range(n_blk) for j in range(N_KV)]
    outputs = [project_out(part) for part in range(out_parts)]
    return attention, outputs


def _mixer_kernel(table_ref, sink_ref, bucket_ref, ind_ref, xc_ref, xl_ref, norm1_ref,
                  w_in_ref, q_gain_ref, k_gain_ref, v_gain_ref, w_s_ref, b_s_ref, a_gain_ref,
                  g_gain_ref, w_o_ref, out_ref, bias_sc, att_sc, gu_sc, gv_sc, *stage_sc,
                  blocks_per_seq, n_steps):
    t = pl.program_id(0)
    last = n_steps - 1
    set_a, set_b = stage_sc[:4], stage_sc[4:]

    lag = jnp.maximum(t - 1, 0)
    pos = lax.rem(lag, blocks_per_seq)
    seq_first = pos == 0
    seq_last = pos == blocks_per_seq - 1

    def pieces(write_set, read_set):
        front = _front_pieces(
            xc_ref, norm1_ref, w_in_ref, q_gain_ref, k_gain_ref, v_gain_ref, w_s_ref, b_s_ref,
            g_gain_ref, gu_sc, gv_sc, read_set[1], read_set[2], *write_set)
        back = _back_pieces(xl_ref, sink_ref, ind_ref, a_gain_ref, w_o_ref, bias_sc,
                            *read_set, att_sc, out_ref, seq_first, seq_last)
        return front, back

    @pl.when(t == 0)
    def _first():
        _build_bias(table_ref, bucket_ref, bias_sc)
        for ref in set_b[1:3]:
            ref[...] = jnp.zeros(ref.shape, ref.dtype)
        (early, gates, query_tiles), _ = pieces(set_a, set_b)
        for piece in early + query_tiles + gates:
            piece()

    @pl.when(t == last)
    def _drain():
        read_set = set_b if last % 2 == 0 else set_a
        write_set = set_a if last % 2 == 0 else set_b
        _, (attention, outputs) = pieces(write_set, read_set)
        for scores, values in attention:
            scores()
            values()
        for project_out in outputs:
            project_out()

    def run(write_set, read_set):
        (early, gates, query_tiles), (attention, outputs) = pieces(write_set, read_set)
        n_items = len(attention)
        between = [lambda: None] * n_items
        between[0] = early[0]
        stride = n_items // len(early[1:])
        for k, piece in enumerate(early[1:]):
            between[1 + k * stride] = piece
        for i in range(n_items + 1):
            if i < n_items:
                attention[i][0]()
                between[i]()
            if i >= 1:
                attention[i - 1][1]()
        w_first, w_last = outputs
        q_first, q_last = query_tiles
        for piece in (w_first, gates[0], q_first, gates[1], q_last, *gates[2:], w_last):
            piece()

    steady = jnp.logical_and(t > 0, t < last)
    parity = lax.rem(t, 2)

    @pl.when(jnp.logical_and(steady, parity == 0))
    def _even():
        run(set_a, set_b)

    @pl.when(jnp.logical_and(steady, parity == 1))
    def _odd():
        run(set_b, set_a)


def _ffn_kernel(x_ref, norm2_ref, w_gate_ref, w_up_ref, w_down_ref, out_ref):
    f32 = jnp.float32
    x = x_ref[...]
    h = _row_rms(x, norm2_ref[:1, :]).astype(jnp.bfloat16)
    d_ff = w_gate_ref.shape[1]
    tiles = d_ff // (2 * LANES)
    bounds = [2 * LANES * (tiles * k // FFN_CHUNKS) for k in range(FFN_CHUNKS)] + [d_ff]
    chunks = [slice(lo, hi) for lo, hi in zip(bounds[:-1], bounds[1:])]

    def activate(cols):
        gate = jnp.dot(h, w_gate_ref[:, cols], preferred_element_type=f32)
        up = jnp.dot(h, w_up_ref[:, cols], preferred_element_type=f32)
        return (gate * (1.0 / (1.0 + jnp.exp(-gate))) * up).astype(jnp.bfloat16)

    y = x
    act = activate(chunks[0])
    for k, cols in enumerate(chunks):
        nxt = activate(chunks[k + 1]) if k + 1 < len(chunks) else None
        y = y + jnp.dot(act, w_down_ref[cols, :], preferred_element_type=f32)
        act = nxt
    out_ref[...] = y


def _resident(shape):
    zeros = (0,) * len(shape)
    return pl.BlockSpec(shape, lambda *_: zeros, pipeline_mode=pl.Buffered(1))


def _mixer(x, p):
    batch, seq, d = x.shape
    tb = MIX_TOKENS
    assert seq % tb == 0 and tb % (2 * BLK) == 0 and seq // BLK >= 2 and d == D_MODEL
    n_i = seq // tb
    n_t = batch * n_i
    assert n_t >= 3

    def cur(t):
        tc = jnp.minimum(t, n_t - 1)
        return lax.div(tc, n_i), lax.rem(tc, n_i)

    def lagged(t):
        tl = jnp.maximum(t - 1, 0)
        return lax.div(tl, n_i), lax.rem(tl, n_i)

    def cur_map(t):
        b, i = cur(t)
        return b, i, 0

    def lag_map(t):
        b, i = lagged(t)
        return b, i, 0

    smem = pl.BlockSpec(memory_space=pltpu.SMEM)
    in_specs = [
        smem,
        smem,
        _resident((BLK, 3 * BLK)),
        _resident((6 * BLK, LANES)),
        pl.BlockSpec((1, tb, d), cur_map),
        pl.BlockSpec((1, tb, d), lag_map),
        _resident((SUBLANES, d)),
        _resident((d, IN_WIDTH)),
        _resident((SUBLANES, ATT_WIDTH)),
        _resident((SUBLANES, ATT_WIDTH)),
        _resident((SUBLANES, GM_WIDTH)),
        _resident((PAIRS, BLK, 2 * BLK)),
        _resident((PAIRS, BLK, LANES)),
        _resident((SUBLANES, ATT_WIDTH)),
        _resident((SUBLANES, GM_WIDTH)),
        _resident((d, d)),
    ]
    stage = [
        pltpu.VMEM((tb, ATT_WIDTH), jnp.bfloat16),
        pltpu.VMEM((2 * N_KV, tb + 2 * BLK, LANES), jnp.bfloat16),
        pltpu.VMEM((2 * N_KV, tb + 2 * BLK, LANES), jnp.bfloat16),
        pltpu.VMEM((tb, GM_WIDTH), jnp.bfloat16),
    ]
    scratch = [
        pltpu.VMEM((3, N_HEADS, BLK, 3 * BLK), jnp.float32),
        pltpu.VMEM((tb, ATT_WIDTH), jnp.float32),
        pltpu.VMEM((tb, GM_WIDTH), jnp.float32),
        pltpu.VMEM((tb, GM_WIDTH), jnp.float32),
    ] + stage + stage
    return pl.pallas_call(
        functools.partial(_mixer_kernel, blocks_per_seq=n_i, n_steps=n_t + 1),
        grid=(n_t + 1,),
        in_specs=in_specs,
        out_specs=pl.BlockSpec((1, tb, d), lag_map),
        out_shape=jax.ShapeDtypeStruct(x.shape, x.dtype),
        scratch_shapes=scratch,
        compiler_params=pltpu.CompilerParams(
            dimension_semantics=("arbitrary",),
            vmem_limit_bytes=VMEM_LIMIT_BYTES),
        name="mixer",
    )(p["table"], p["sink"], p["bucket"], p["indicator"], x, x, p["norm1"], p["w_in"], p["q_gain"],
      p["k_gain"], p["v_gain"], p["w_s"], p["b_s"], p["a_gain"], p["g_gain"], p["w_o"])


def _ffn(x2d, p):
    n, d = x2d.shape
    tm = FFN_TOKENS
    assert n % tm == 0
    d_ff = p["w_gate"].shape[1]
    return pl.pallas_call(
        _ffn_kernel,
        grid=(n // tm,),
        in_specs=[
            pl.BlockSpec((tm, d), lambda i: (i, 0)),
            _resident((SUBLANES, d)),
            _resident((d, d_ff)),
            _resident((d, d_ff)),
            _resident((d_ff, d)),
        ],
        out_specs=pl.BlockSpec((tm, d), lambda i: (i, 0)),
        out_shape=jax.ShapeDtypeStruct(x2d.shape, x2d.dtype),
        compiler_params=pltpu.CompilerParams(
            dimension_semantics=("arbitrary",),
            vmem_limit_bytes=VMEM_LIMIT_BYTES),
        name="ffn",
    )(x2d, p["norm2"], p["w_gate"], p["w_up"], p["w_down"])


def _layer_params(rel_bias_table, norm1, w_in, q_gain, k_gain, sink, v_gain, w_s, b_s,
                  attn_out_gain, gmlp_out_gain, w_o, norm2, w_gate, w_up, w_down):
    bf16 = jnp.bfloat16

    def row(v, width=None):
        v = v.reshape(1, -1).astype(jnp.float32)
        if width is not None:
            v = jnp.pad(v, ((0, 0), (0, width - v.shape[1])))
        return jnp.broadcast_to(v, (SUBLANES, v.shape[1]))

    w_pairs = jnp.concatenate([w_s[0::2], w_s[1::2]], axis=-1).astype(bf16)
    b_pairs = jnp.repeat(jnp.transpose(b_s.reshape(PAIRS, 2, BLK), (0, 2, 1)), HEAD_DIM, axis=-1)
    return {
        "table": rel_bias_table.astype(jnp.float32),
        "sink": sink.astype(jnp.float32),
        "bucket": jnp.asarray(_t5_bucket_table()),
        "indicator": jnp.asarray(_key_set_indicator(), dtype=bf16),
        "norm1": row(norm1),
        "w_in": w_in.astype(bf16),
        "q_gain": row(jnp.tile(q_gain, N_HEADS)),
        "k_gain": row(jnp.tile(k_gain, N_KV), ATT_WIDTH),
        "v_gain": row(v_gain),
        "w_s": w_pairs,
        "b_s": b_pairs.astype(jnp.float32),
        "a_gain": row(attn_out_gain),
        "g_gain": row(gmlp_out_gain),
        "w_o": w_o.astype(bf16),
        "norm2": row(norm2),
        "w_gate": w_gate.astype(bf16),
        "w_up": w_up.astype(bf16),
        "w_down": w_down.astype(bf16),
    }


def kernel(x_prompt, x_sample, rel_bias_table, norm1, w_in, q_gain, k_gain, sink, v_gain, w_s, b_s,
           attn_out_gain, gmlp_out_gain, w_o, norm2, w_gate, w_up, w_down):
    layers = [
        _layer_params(rel_bias_table, norm1[l], w_in[l], q_gain[l], k_gain[l], sink[l], v_gain[l],
                      w_s[l], b_s[l], attn_out_gain[l], gmlp_out_gain[l], w_o[l], norm2[l],
                      w_gate[l], w_up[l], w_down[l])
        for l in range(norm1.shape[0])]
    outs = []
    for x in (x_prompt, x_sample):
        for p in layers:
            x = _mixer(x, p)
            x = _ffn(x.reshape(-1, D_MODEL), p).reshape(x.shape)
        outs.append(x)
    return tuple(outs)
```
